```python
import math
import jax, jax.numpy as jnp
from jax import lax
import numpy as np

D_MODEL = 1024
BATCH = 8
SEQ = 2048
DEPTH = 2

MEM_LEN = 256
HEAD_DIM = 64
DA_HEADS = 4
DA_V_DIM = 2 * HEAD_DIM
GQ_HEADS = 8
GQ_KV_HEADS = 2
GQ_REP = GQ_HEADS // GQ_KV_HEADS
MX_HEADS = 4
MX_HEAD_DIM = 128
BRANCH_W = 512
N_BRANCHES = 3
GRID_W = 64
ROPE_THETA = 10000.0
Q_BLOCK = 128
NORM_EPS = 1e-6
N_GROUPS = 4
EXPERTS_PER_GROUP = 8
N_EXPERTS = N_GROUPS * EXPERTS_PER_GROUP
TOP_K = 2
D_EXPERT = 512
MOE_BLOCK = 128
IN_WIDTHS = (DA_HEADS * 2 * HEAD_DIM,
             DA_HEADS * 2 * HEAD_DIM,
             DA_HEADS * DA_V_DIM,
             GQ_HEADS * HEAD_DIM,
             GQ_KV_HEADS * HEAD_DIM,
             GQ_KV_HEADS * HEAD_DIM,
             MX_HEADS * MX_HEAD_DIM,
             N_BRANCHES * D_MODEL)
D_IN = sum(IN_WIDTHS)

kernel_name = 'hybrid_diffattn_axialgqa_memxattn_hiermoe'

F32 = jnp.float32


def rms_norm(x, g):
    xf = x.astype(F32)
    y = xf * lax.rsqrt(jnp.mean(xf * xf, axis=-1, keepdims=True) + NORM_EPS)
    return (y * g.astype(F32)).astype(x.dtype)


def rope_angles_1d(S):
    pos = jnp.arange(S, dtype=F32)
    inv = ROPE_THETA ** (-jnp.arange(0, HEAD_DIM, 2, dtype=F32) / HEAD_DIM)
    return pos[:, None] * inv[None, :]


def rope_angles_axial(S):
    rows = S // GRID_W
    row = jnp.broadcast_to(jnp.arange(rows, dtype=F32)[:, None], (rows, GRID_W)).reshape(-1)
    col = jnp.broadcast_to(jnp.arange(GRID_W, dtype=F32)[None, :], (rows, GRID_W)).reshape(-1)
    axis_dim = HEAD_DIM // 2
    inv = ROPE_THETA ** (-jnp.arange(0, axis_dim, 2, dtype=F32) / axis_dim)
    return jnp.concatenate([row[:, None] * inv, col[:, None] * inv], axis=-1)


def apply_rope(x, ang):
    S, half = ang.shape
    shape = (S,) + (1,) * (x.ndim - 3) + (half,)
    c = jnp.cos(ang).reshape(shape)
    s = jnp.sin(ang).reshape(shape)
    xf = x.astype(F32)
    x1, x2 = xf[..., :half], xf[..., half:]
    return jnp.concatenate([x1 * c - x2 * s, x2 * c + x1 * s], axis=-1).astype(x.dtype)


def to_blocks(t):
    B, S = t.shape[:2]
    return t.reshape((B, S // Q_BLOCK, Q_BLOCK) + t.shape[2:]).swapaxes(0, 1)


def from_blocks(t):
    nb, B, qb = t.shape[:3]
    return t.swapaxes(0, 1).reshape((B, nb * qb) + t.shape[3:])


def diff_attention(q, k, v, lam):
    scale = HEAD_DIM ** -0.5

    def one_block(qb):
        s = jnp.einsum('bqhcd,bkhcd->bhcqk', qb, k).astype(F32) * scale
        p = jax.nn.softmax(s, axis=-1)
        a = p[:, :, 0] - lam * p[:, :, 1]
        return jnp.einsum('bhqk,bkhe->bqhe', a.astype(v.dtype), v)

    return from_blocks(lax.map(one_block, to_blocks(q)))


def gqa_attention(q, k, v):
    scale = HEAD_DIM ** -0.5

    def one_block(qb):
        s = jnp.einsum('bqnrd,bsnd->bnrqs', qb, k).astype(F32) * scale
        p = jax.nn.softmax(s, axis=-1)
        return jnp.einsum('bnrqs,bsnd->bqnrd', p.astype(v.dtype), v)

    return from_blocks(lax.map(one_block, to_blocks(q)))


def memory_attention(q, k, v):
    s = jnp.einsum('bqhe,bmhe->bhqm', q, k).astype(F32) * (MX_HEAD_DIM ** -0.5)
    p = jax.nn.softmax(s, axis=-1)
    return jnp.einsum('bhqm,bmhe->bqhe', p.astype(v.dtype), v)


def hier_route(t, w_rg, b_rg, w_re, b_re):
    g_prob = jax.nn.softmax((t @ w_rg + b_rg).astype(F32), axis=-1)
    g_idx = jnp.argmax(g_prob, axis=-1)
    g_p = jnp.max(g_prob, axis=-1)
    e_logits = (t @ w_re + b_re).astype(F32).reshape(-1, N_GROUPS, EXPERTS_PER_GROUP)
    in_group = jnp.take_along_axis(e_logits, g_idx[:, None, None], axis=1)[:, 0]
    e_prob = jax.nn.softmax(in_group, axis=-1)
    top_p, top_i = lax.top_k(e_prob, TOP_K)
    top_p = top_p / jnp.sum(top_p, axis=-1, keepdims=True)
    weights = g_p[:, None] * top_p
    experts = (g_idx[:, None] * EXPERTS_PER_GROUP + top_i).astype(jnp.int32)
    return experts, weights


def moe_forward(t, experts, weights, w_gate, w_up, w_down):
    T, D = t.shape
    A = T * TOP_K
    P = A + N_EXPERTS * MOE_BLOCK
    NB = P // MOE_BLOCK
    flat_e = experts.reshape(-1)
    flat_tok = jnp.arange(A, dtype=jnp.int32) // TOP_K
    flat_w = weights.reshape(-1)
    order = jnp.argsort(flat_e)
    se, stok, sw = flat_e[order], flat_tok[order], flat_w[order]
    counts = jnp.bincount(flat_e, length=N_EXPERTS)
    starts = jnp.cumsum(counts) - counts
    padded = (counts + MOE_BLOCK - 1) // MOE_BLOCK * MOE_BLOCK
    pends = jnp.cumsum(padded)
    pstarts = pends - padded
    dest = pstarts[se] + (jnp.arange(A, dtype=jnp.int32) - starts[se])
    buf = jnp.zeros((P, D), t.dtype).at[dest].set(t[stok])
    block_start = jnp.arange(NB, dtype=jnp.int32) * MOE_BLOCK
    block_e = jnp.minimum(jnp.searchsorted(pends, block_start, side='right'), N_EXPERTS - 1)

    def run(args):
        xb, e = args
        hid = jax.nn.silu(xb @ w_gate[e]) * (xb @ w_up[e])
        return hid @ w_down[e]

    ybuf = lax.map(run, (buf.reshape(NB, MOE_BLOCK, D), block_e)).reshape(P, D)
    y = ybuf[dest] * sw[:, None].astype(t.dtype)
    return jax.ops.segment_sum(y, stok, num_segments=T)


def setup_inputs(seed: int = 0) -> dict:
    key = jax.random.key(seed)
    ks = jax.random.split(key, 32)

    def nrm(k, shape, scale):
        return jax.random.normal(k, shape, F32) * scale

    def gain(k, shape):
        return 1.0 + 0.02 * jax.random.normal(k, shape, F32)

    D = D_MODEL
    return {
        'x': nrm(ks[0], (BATCH, SEQ, D), 1.0),
        'mem': nrm(ks[1], (BATCH, MEM_LEN, D), 1.0),
        'mem_norm_g': gain(ks[2], (D,)),
        'w_mem_kv': nrm(ks[3], (D, 2 * MX_HEADS * MX_HEAD_DIM), D ** -0.5),
        'norm1_g': gain(ks[4], (DEPTH, D)),
        'w_in': nrm(ks[5], (DEPTH, D, D_IN), D ** -0.5),
        'lam_q1': nrm(ks[6], (DEPTH, HEAD_DIM), 0.1),
        'lam_k1': nrm(ks[7], (DEPTH, HEAD_DIM), 0.1),
        'lam_q2': nrm(ks[8], (DEPTH, HEAD_DIM), 0.1),
        'lam_k2': nrm(ks[9], (DEPTH, HEAD_DIM), 0.1),
        'subln_g': gain(ks[10], (DEPTH, DA_V_DIM)),
        'q_norm_g': gain(ks[11], (DEPTH, HEAD_DIM)),
        'k_norm_g': gain(ks[12], (DEPTH, HEAD_DIM)),
        'w_up_a': nrm(ks[13], (DEPTH, BRANCH_W, D), BRANCH_W ** -0.5),
        'w_up_b': nrm(ks[14], (DEPTH, BRANCH_W, D), BRANCH_W ** -0.5),
        'w_up_c': nrm(ks[15], (DEPTH, BRANCH_W, D), BRANCH_W ** -0.5),
        'w_out': nrm(ks[16], (DEPTH, D, D), D ** -0.5),
        'norm2_g': gain(ks[17], (DEPTH, D)),
        'w_router_group': nrm(ks[18], (DEPTH, D, N_GROUPS), D ** -0.5),
        'b_router_group': nrm(ks[19], (DEPTH, N_GROUPS), 0.01),
        'w_router_expert': nrm(ks[20], (DEPTH, D, N_EXPERTS), D ** -0.5),
        'b_router_expert': nrm(ks[21], (DEPTH, N_EXPERTS), 0.01),
        'w_exp_gate': nrm(ks[22], (DEPTH, N_EXPERTS, D, D_EXPERT), D ** -0.5),
        'w_exp_up': nrm(ks[23], (DEPTH, N_EXPERTS, D, D_EXPERT), D ** -0.5),
        'w_exp_down': nrm(ks[24], (DEPTH, N_EXPERTS, D_EXPERT, D), D_EXPERT ** -0.5),
        'final_norm_g': gain(ks[25], (D,)),
    }


def reference(x, mem, mem_norm_g, w_mem_kv, norm1_g, w_in, lam_q1, lam_k1, lam_q2, lam_k2,
              subln_g, q_norm_g, k_norm_g, w_up_a, w_up_b, w_up_c, w_out, norm2_g,
              w_router_group, b_router_group, w_router_expert, b_router_expert,
              w_exp_gate, w_exp_up, w_exp_down, final_norm_g):
    B, S, D = x.shape
    ang_1d = rope_angles_1d(S)
    ang_2d = rope_angles_axial(S)
    split_idx = [int(i) for i in np.cumsum(IN_WIDTHS)[:-1]]

    mem_kv = rms_norm(mem, mem_norm_g) @ w_mem_kv
    mk, mv = jnp.split(mem_kv, 2, axis=-1)
    M = mem.shape[1]
    mk = mk.reshape(B, M, MX_HEADS, MX_HEAD_DIM)
    mv = mv.reshape(B, M, MX_HEADS, MX_HEAD_DIM)

    for l in range(DEPTH):
        h = rms_norm(x, norm1_g[l])
        proj = h @ w_in[l]
        aq, ak, av, bq, bk, bv, cq, gl = jnp.split(proj, split_idx, axis=-1)

        lam_init = 0.8 - 0.6 * math.exp(-0.3 * l)
        lam = (jnp.exp(jnp.sum(lam_q1[l].astype(F32) * lam_k1[l].astype(F32)))
               - jnp.exp(jnp.sum(lam_q2[l].astype(F32) * lam_k2[l].astype(F32))) + lam_init)
        aq = apply_rope(aq.reshape(B, S, DA_HEADS, 2, HEAD_DIM), ang_1d)
        ak = apply_rope(ak.reshape(B, S, DA_HEADS, 2, HEAD_DIM), ang_1d)
        av = av.reshape(B, S, DA_HEADS, DA_V_DIM)
        oa = diff_attention(aq, ak, av, lam)
        oa = rms_norm(oa, subln_g[l]) * (1.0 - lam_init)

        bq = apply_rope(rms_norm(bq.reshape(B, S, GQ_HEADS, HEAD_DIM), q_norm_g[l]), ang_2d)
        bk = apply_rope(rms_norm(bk.reshape(B, S, GQ_KV_HEADS, HEAD_DIM), k_norm_g[l]), ang_2d)
        bq = bq.reshape(B, S, GQ_KV_HEADS, GQ_REP, HEAD_DIM)
        bv = bv.reshape(B, S, GQ_KV_HEADS, HEAD_DIM)
        ob = gqa_attention(bq, bk, bv)

        oc = memory_attention(cq.reshape(B, S, MX_HEADS, MX_HEAD_DIM), mk, mv)

        ya = oa.reshape(B, S, BRANCH_W) @ w_up_a[l]
        yb = ob.reshape(B, S, BRANCH_W) @ w_up_b[l]
        yc = oc.reshape(B, S, BRANCH_W) @ w_up_c[l]
        g = jax.nn.sigmoid(gl.astype(F32)).astype(x.dtype).reshape(B, S, N_BRANCHES, D)
        merged = g[:, :, 0] * ya + g[:, :, 1] * yb + g[:, :, 2] * yc
        x = x + merged @ w_out[l]

        h2 = rms_norm(x, norm2_g[l]).reshape(B * S, D)
        experts, weights = hier_route(h2, w_router_group[l], b_router_group[l],
                                      w_router_expert[l], b_router_expert[l])
        y = moe_forward(h2, experts, weights, w_exp_gate[l], w_exp_up[l], w_exp_down[l])
        x = x + y.reshape(B, S, D)

    return rms_norm(x, final_norm_g)
```

```python
import functools
import math

import jax
import jax.numpy as jnp
from jax import lax
from jax.experimental import pallas as pl
from jax.experimental.pallas import tpu as pltpu

F32 = jnp.float32
BF16 = jnp.bfloat16

D_MODEL = 1024
SEQ = 2048
MEM_LEN = 256
HEAD_DIM = 64
MX_HEAD_DIM = 128
BRANCH_W = 512
GRID_W = 64
ROPE_THETA = 10000.0
NORM_EPS = 1e-6
N_GROUPS = 4
EXPERTS_PER_GROUP = 8
N_EXPERTS = N_GROUPS * EXPERTS_PER_GROUP
TOP_K = 2
D_EXPERT = 512

LANES = 128
SUBLANES = 8
CHUNKS = D_MODEL // LANES

TM_IN = 256
TQ = 256
TK = 512
TM_POST = 256
BM = 256
TM_COMB = 256
VMEM_LIMIT = 56 * 1024 * 1024

C_AQ, C_AK, C_AV, C_BQ, C_BK, C_BV, C_CQ, C_G, C_END = (
    0, 512, 1024, 1536, 2048, 2304, 2560, 3072, 6144)
ROUTE_ROWS = 40


def _rms(xf, g):
    ms = jnp.mean(xf * xf, axis=-1, keepdims=True)
    return xf * lax.rsqrt(ms + NORM_EPS) * g


def _dot(a, b):
    return jnp.dot(a, b, preferred_element_type=F32)


def _dot_nt(a, b):
    return lax.dot_general(a, b, (((1,), (1,)), ((), ())), preferred_element_type=F32)


def _memkv_kernel(m_ref, g_ref, w_ref, o_ref):
    h = _rms(m_ref[...], g_ref[...]).astype(BF16)
    o_ref[...] = _dot(h, w_ref[...]).astype(BF16)


def _memkv(mem2d, g, w):
    n = mem2d.shape[0]
    tm = 512
    return pl.pallas_call(
        _memkv_kernel,
        out_shape=jax.ShapeDtypeStruct((n, w.shape[1]), BF16),
        grid=(n // tm,),
        in_specs=[pl.BlockSpec((tm, D_MODEL), lambda i: (i, 0)),
                  pl.BlockSpec((1, D_MODEL), lambda i: (0, 0)),
                  pl.BlockSpec(w.shape, lambda i: (0, 0))],
        out_specs=pl.BlockSpec((tm, w.shape[1]), lambda i: (i, 0)),
        compiler_params=pltpu.CompilerParams(
            dimension_semantics=("arbitrary",), vmem_limit_bytes=VMEM_LIMIT),
        name="memkv",
    )(mem2d, g, w)


def _in_kernel(x_ref, g1_ref, w_ref, ca_ref, sa_ref, cb_ref, sb_ref, qg_ref, kg_ref, bd_ref,
               aq_ref, ak_ref, av_ref, bq_ref, bk_ref, bv_ref, cq_ref, gate_ref):
    tm = x_ref.shape[0]
    h = _rms(x_ref[...], g1_ref[...]).astype(BF16)
    lane = lax.broadcasted_iota(jnp.int32, (tm, LANES), 1)
    first_half = (lane & (HEAD_DIM // 2)) == 0

    def seg(lo, hi):
        return _dot(h, w_ref[:, lo:hi])

    def rope(p, c, s):
        sw = jnp.where(first_half, pltpu.roll(p, LANES - HEAD_DIM // 2, 1),
                       pltpu.roll(p, HEAD_DIM // 2, 1))
        return p * c + sw * s

    def group_norm(p, gain):
        n = p.shape[1]
        ss = _dot((p * p).astype(BF16), bd_ref[:n, :n])
        return p * lax.rsqrt(ss * (1.0 / HEAD_DIM) + NORM_EPS) * gain

    def rope_store(p, c_ref, s_ref, o_ref, scale):
        c = c_ref[...]
        s = s_ref[...]
        for j in range(p.shape[1] // LANES):
            sl = slice(j * LANES, (j + 1) * LANES)
            o_ref[:, sl] = (rope(p[:, sl], c, s) * scale).astype(BF16)

    q_scale = HEAD_DIM ** -0.5
    rope_store(seg(C_AQ, C_AK), ca_ref, sa_ref, aq_ref, q_scale)
    rope_store(seg(C_AK, C_AV), ca_ref, sa_ref, ak_ref, 1.0)
    av_ref[...] = seg(C_AV, C_BQ).astype(BF16)
    rope_store(group_norm(seg(C_BQ, C_BK), qg_ref[...]), cb_ref, sb_ref, bq_ref, q_scale)
    rope_store(group_norm(seg(C_BK, C_BV), kg_ref[...]), cb_ref, sb_ref, bk_ref, 1.0)
    bv_ref[...] = seg(C_BV, C_CQ).astype(BF16)
    cq_ref[...] = (seg(C_CQ, C_G) * (MX_HEAD_DIM ** -0.5)).astype(BF16)
    for j in range((C_END - C_G) // 512):
        lo = C_G + j * 512
        z = seg(lo, lo + 512)
        gate_ref[:, j * 512:(j + 1) * 512] = (1.0 / (1.0 + jnp.exp(-z))).astype(BF16)


def _in_proj(x2d, g1, w, ca, sa, cb, sb, qg, kg, bd):
    t = x2d.shape[0]
    tm = TM_IN
    nrb = SEQ // tm
    row = lambda i: (i, 0)
    const = lambda i: (0, 0)
    tab = lambda i: (i % nrb, 0)
    widths = (512, 512, 512, 512, 256, 256, 512, 3072)
    return pl.pallas_call(
        _in_kernel,
        out_shape=[jax.ShapeDtypeStruct((t, n), BF16) for n in widths],
        grid=(t // tm,),
        in_specs=[pl.BlockSpec((tm, D_MODEL), row),
                  pl.BlockSpec((1, D_MODEL), const),
                  pl.BlockSpec(w.shape, const),
                  pl.BlockSpec((tm, LANES), tab), pl.BlockSpec((tm, LANES), tab),
                  pl.BlockSpec((tm, LANES), tab), pl.BlockSpec((tm, LANES), tab),
                  pl.BlockSpec((1, 512), const), pl.BlockSpec((1, 256), const),
                  pl.BlockSpec((512, 512), const)],
        out_specs=[pl.BlockSpec((tm, n), row) for n in widths],
        compiler_params=pltpu.CompilerParams(
            dimension_semantics=("arbitrary",), vmem_limit_bytes=VMEM_LIMIT),
        name="in_proj",
    )(x2d, g1, w, ca, sa, cb, sb, qg, kg, bd)


def _attn_kernel(*refs, diff, post_scale, lam_init):
    if diff:
        lamp_ref, gs_ref, q_ref, k_ref, v_ref, o_ref = refs
    else:
        q_ref, k_ref, v_ref, o_ref = refs
    tq = q_ref.shape[0]
    q = q_ref[...]
    lane = lax.broadcasted_iota(jnp.int32, (tq, LANES), 1)
    lo = lane < HEAD_DIM
    zero = jnp.zeros_like(q)
    qs = jnp.concatenate([jnp.where(lo, q, zero), jnp.where(lo, zero, q)], axis=0)
    rows = 2 * tq
    m = jnp.full((rows, 1), -jnp.inf, F32)
    l = jnp.zeros((rows, 1), F32)
    acc = jnp.zeros((rows, LANES), F32)
    for j in range(SEQ // TK):
        kj = k_ref[j * TK:(j + 1) * TK, :]
        vj = v_ref[j * TK:(j + 1) * TK, :]
        s = _dot_nt(qs, kj)
        m_new = jnp.maximum(m, jnp.max(s, axis=-1, keepdims=True))
        alpha = jnp.exp(m - m_new)
        e = jnp.exp(s - m_new)
        l = alpha * l + jnp.sum(e, axis=-1, keepdims=True)
        acc = alpha * acc + _dot(e.astype(BF16), vj)
        m = m_new
    o = acc / l
    if diff:
        lp = lamp_ref[...]
        lam = (jnp.exp(jnp.sum(lp[0:1] * lp[1:2], axis=-1, keepdims=True))
               - jnp.exp(jnp.sum(lp[2:3] * lp[3:4], axis=-1, keepdims=True)) + lam_init)
        d = o[:tq] - lam * o[tq:]
        out = _rms(d, gs_ref[...]) * post_scale
    else:
        out = jnp.where(lo, o[:tq], o[tq:])
    o_ref[...] = out.astype(BF16)


def _attention(q, k, v, batch, *, diff, lamp=None, gs=None, lam_init=0.0):
    t = q.shape[0]
    nq = SEQ // TQ
    nblk = q.shape[1] // LANES
    kv_per = nblk // (k.shape[1] // LANES)
    qmap = lambda b, h, i: (b * nq + i, h)
    kmap = lambda b, h, i: (b, h // kv_per)
    in_specs = [pl.BlockSpec((TQ, LANES), qmap),
                pl.BlockSpec((SEQ, LANES), kmap),
                pl.BlockSpec((SEQ, LANES), kmap)]
    args = [q, k, v]
    if diff:
        const = lambda b, h, i: (0, 0)
        in_specs = [pl.BlockSpec((4, HEAD_DIM), const), pl.BlockSpec((1, LANES), const)] + in_specs
        args = [lamp, gs] + args
    return pl.pallas_call(
        functools.partial(_attn_kernel, diff=diff, post_scale=1.0 - lam_init, lam_init=lam_init),
        out_shape=jax.ShapeDtypeStruct((t, q.shape[1]), BF16),
        grid=(batch, nblk, nq),
        in_specs=in_specs,
        out_specs=pl.BlockSpec((TQ, LANES), qmap),
        compiler_params=pltpu.CompilerParams(
            dimension_semantics=("arbitrary", "arbitrary", "arbitrary"),
            vmem_limit_bytes=VMEM_LIMIT),
        name="attn_diff" if diff else "attn_gqa",
    )(*args)


def _post_kernel(x_ref, oa_ref, ob_ref, cq_ref, gate_ref, mkv_ref, wa_ref, wb_ref, wc_ref,
                 wo_ref, g2_ref, wr_ref, br_ref, tri_ref,
                 xo_ref, h2_ref, route_ref, cnt_ref, carry_ref):
    tm = x_ref.shape[0]
    i = pl.program_id(0)

    @pl.when(i == 0)
    def _():
        carry_ref[...] = jnp.zeros_like(carry_ref)

    heads = []
    for hd in range(BRANCH_W // MX_HEAD_DIM):
        sl = slice(hd * MX_HEAD_DIM, (hd + 1) * MX_HEAD_DIM)
        sv = slice(BRANCH_W + hd * MX_HEAD_DIM, BRANCH_W + (hd + 1) * MX_HEAD_DIM)
        s = _dot_nt(cq_ref[:, sl], mkv_ref[:, sl])
        e = jnp.exp(s - jnp.max(s, axis=-1, keepdims=True))
        den = jnp.sum(e, axis=-1, keepdims=True)
        heads.append((_dot(e.astype(BF16), mkv_ref[:, sv]) / den).astype(BF16))
    oc = jnp.concatenate(heads, axis=1)

    ya = _dot(oa_ref[...], wa_ref[...])
    yb = _dot(ob_ref[...], wb_ref[...])
    yc = _dot(oc, wc_ref[...])
    merged = (gate_ref[:, 0:D_MODEL].astype(F32) * ya
              + gate_ref[:, D_MODEL:2 * D_MODEL].astype(F32) * yb
              + gate_ref[:, 2 * D_MODEL:3 * D_MODEL].astype(F32) * yc)
    xn = x_ref[...] + _dot(merged.astype(BF16), wo_ref[...])
    xo_ref[...] = xn
    h2 = _rms(xn, g2_ref[...])
    for c in range(CHUNKS):
        h2_ref[pl.ds(c, tm, stride=CHUNKS), :] = h2[:, c * LANES:(c + 1) * LANES]

    h_hi = h2.astype(BF16)
    h_lo = (h2 - h_hi.astype(F32)).astype(BF16)
    l2 = _dot_nt(wr_ref[...], h_hi)
    logits = (l2[:ROUTE_ROWS] + l2[ROUTE_ROWS:] + _dot_nt(wr_ref[:ROUTE_ROWS, :], h_lo)
              + br_ref[...])

    neg = -jnp.inf
    r8 = lax.broadcasted_iota(jnp.int32, (SUBLANES, tm), 0)
    r32 = lax.broadcasted_iota(jnp.int32, (N_EXPERTS, tm), 0)
    gl = jnp.where(r8 < N_GROUPS, logits[0:SUBLANES], neg)
    gmax = jnp.max(gl, axis=0, keepdims=True)
    gidx = jnp.min(jnp.where(gl == gmax, r8, SUBLANES), axis=0, keepdims=True)
    gp = 1.0 / jnp.sum(jnp.exp(gl - gmax), axis=0, keepdims=True)
    el = jnp.where((r32 // EXPERTS_PER_GROUP) == gidx, logits[SUBLANES:ROUTE_ROWS], neg)
    m1 = jnp.max(el, axis=0, keepdims=True)
    i1 = jnp.min(jnp.where(el == m1, r32, N_EXPERTS), axis=0, keepdims=True)
    el2 = jnp.where(r32 == i1, neg, el)
    m2 = jnp.max(el2, axis=0, keepdims=True)
    i2 = jnp.min(jnp.where(el2 == m2, r32, N_EXPERTS), axis=0, keepdims=True)
    d = jnp.exp(m2 - m1)
    w1 = gp / (1.0 + d)
    w2 = gp * d / (1.0 + d)

    hit1 = r32 == i1
    hit2 = r32 == i2
    oh = jnp.where(hit1 | hit2, 1.0, 0.0)
    before = _dot(oh.astype(BF16), tri_ref[...]) + carry_ref[...]
    rank1 = jnp.sum(jnp.where(hit1, before, 0.0), axis=0, keepdims=True)
    rank2 = jnp.sum(jnp.where(hit2, before, 0.0), axis=0, keepdims=True)
    carry_ref[...] = carry_ref[...] + jnp.sum(oh, axis=1, keepdims=True)
    zrow = jnp.zeros_like(w1)
    route_ref[...] = jnp.concatenate(
        [i1.astype(F32), i2.astype(F32), rank1, rank2, w1, w2, zrow, zrow], axis=0)
    cnt_ref[...] = jnp.broadcast_to(carry_ref[...], cnt_ref.shape)


def _post(x2d, oa, ob, cq, gate, mkv, wa, wb, wc, wo, g2, wr, br, tri):
    t = x2d.shape[0]
    tm = TM_POST
    nrb = SEQ // tm
    row = lambda i: (i, 0)
    const = lambda i: (0, 0)
    return pl.pallas_call(
        _post_kernel,
        out_shape=[jax.ShapeDtypeStruct((t, D_MODEL), F32),
                   jax.ShapeDtypeStruct((t * CHUNKS, LANES), F32),
                   jax.ShapeDtypeStruct((SUBLANES, t), F32),
                   jax.ShapeDtypeStruct((N_EXPERTS, LANES), F32)],
        grid=(t // tm,),
        in_specs=[pl.BlockSpec((tm, D_MODEL), row),
                  pl.BlockSpec((tm, BRANCH_W), row),
                  pl.BlockSpec((tm, BRANCH_W), row),
                  pl.BlockSpec((tm, BRANCH_W), row),
                  pl.BlockSpec((tm, 3 * D_MODEL), row),
                  pl.BlockSpec((MEM_LEN, 2 * BRANCH_W), lambda i: (i // nrb, 0)),
                  pl.BlockSpec(wa.shape, const), pl.BlockSpec(wb.shape, const),
                  pl.BlockSpec(wc.shape, const), pl.BlockSpec(wo.shape, const),
                  pl.BlockSpec((1, D_MODEL), const),
                  pl.BlockSpec(wr.shape, const), pl.BlockSpec(br.shape, const),
                  pl.BlockSpec(tri.shape, const)],
        out_specs=[pl.BlockSpec((tm, D_MODEL), row),
                   pl.BlockSpec((tm * CHUNKS, LANES), row),
                   pl.BlockSpec((SUBLANES, tm), lambda i: (0, i)),
                   pl.BlockSpec((N_EXPERTS, LANES), const)],
        scratch_shapes=[pltpu.VMEM((N_EXPERTS, 1), F32)],
        compiler_params=pltpu.CompilerParams(
            dimension_semantics=("arbitrary",), vmem_limit_bytes=VMEM_LIMIT),
        name="post",
    )(x2d, oa, ob, cq, gate, mkv, wa, wb, wc, wo, g2, wr, br, tri)


def _start_row_gather(idx_ref, n, src_ref, dst_ref, slot, sem):
    def body(r, carry):
        tok = idx_ref[0, 0, r]
        pltpu.make_async_copy(
            src_ref.at[pl.ds(pl.multiple_of(tok * CHUNKS, CHUNKS), CHUNKS)],
            dst_ref.at[slot, pl.ds(pl.multiple_of(r * CHUNKS, CHUNKS), CHUNKS)],
            sem.at[slot]).start()
        return carry
    lax.fori_loop(0, n, body, 0)


def _wait_row_gather(n, src_ref, dst_ref, slot, sem):
    pltpu.make_async_copy(src_ref.at[pl.ds(0, n * CHUNKS)], dst_ref.at[slot], sem.at[slot]).wait()


def _rows_from_tiles(buf_ref, slot, first, n):
    return jnp.concatenate(
        [buf_ref[slot, pl.ds(first * CHUNKS + c, n, stride=CHUNKS), :] for c in range(CHUNKS)],
        axis=1)


def _moe_kernel(be_ref, nb_ref, cur_ref, nxt_ref, h2_ref, wg_ref, wu_ref, wd_ref, y_ref,
                xbuf, sem, wgb, wub, wdb):
    b = pl.program_id(0)
    nb = nb_ref[0]
    slot = b % 2

    @pl.when(b == 0)
    def _():
        _start_row_gather(cur_ref, BM, h2_ref, xbuf, 0, sem)

    @pl.when(b + 1 < nb)
    def _():
        _start_row_gather(nxt_ref, BM, h2_ref, xbuf, 1 - slot, sem)

    @pl.when((b == 0) | (be_ref[b] != be_ref[jnp.maximum(b - 1, 0)]))
    def _():
        wgb[...] = wg_ref[0].astype(BF16)
        wub[...] = wu_ref[0].astype(BF16)
        wdb[...] = wd_ref[0].astype(BF16)

    def compute(s):
        _wait_row_gather(BM, h2_ref, xbuf, s, sem)
        xb = _rows_from_tiles(xbuf, s, 0, BM).astype(BF16)
        hg = _dot(xb, wgb[...])
        hu = _dot(xb, wub[...])
        hid = (hg / (1.0 + jnp.exp(-hg)) * hu).astype(BF16)
        y = _dot(hid, wdb[...])
        for c in range(CHUNKS):
            y_ref[pl.ds(c, BM, stride=CHUNKS), :] = y[:, c * LANES:(c + 1) * LANES]

    for s in range(2):
        @pl.when((b < nb) & (slot == s))
        def _(s=s):
            compute(s)

    @pl.when(b >= nb)
    def _():
        y_ref[...] = jnp.zeros_like(y_ref)


def _moe(block_e, nb_used, src, h2, wg, wu, wd):
    nblk = src.shape[0]
    return pl.pallas_call(
        _moe_kernel,
        out_shape=jax.ShapeDtypeStruct((nblk * BM * CHUNKS, LANES), F32),
        grid_spec=pltpu.PrefetchScalarGridSpec(
            num_scalar_prefetch=2,
            grid=(nblk,),
            in_specs=[
                pl.BlockSpec((1, 1, BM), lambda b, be, nb: (b, 0, 0), memory_space=pltpu.SMEM),
                pl.BlockSpec((1, 1, BM), lambda b, be, nb: (jnp.minimum(b + 1, nblk - 1), 0, 0),
                             memory_space=pltpu.SMEM),
                pl.BlockSpec(memory_space=pl.ANY),
                pl.BlockSpec((1, D_MODEL, D_EXPERT), lambda b, be, nb: (be[b], 0, 0)),
                pl.BlockSpec((1, D_MODEL, D_EXPERT), lambda b, be, nb: (be[b], 0, 0)),
                pl.BlockSpec((1, D_EXPERT, D_MODEL), lambda b, be, nb: (be[b], 0, 0)),
            ],
            out_specs=pl.BlockSpec((BM * CHUNKS, LANES), lambda b, be, nb: (b, 0)),
            scratch_shapes=[pltpu.VMEM((2, BM * CHUNKS, LANES), F32),
                            pltpu.SemaphoreType.DMA((2,)),
                            pltpu.VMEM((D_MODEL, D_EXPERT), BF16),
                            pltpu.VMEM((D_MODEL, D_EXPERT), BF16),
                            pltpu.VMEM((D_EXPERT, D_MODEL), BF16)]),
        compiler_params=pltpu.CompilerParams(
            dimension_semantics=("arbitrary",), vmem_limit_bytes=VMEM_LIMIT),
        name="experts",
    )(block_e, nb_used, src, src, h2, wg, wu, wd)


def _comb_kernel(cur_ref, nxt_ref, x_ref, wt_ref, fg_ref, y_ref, o_ref, ybuf, sem, *, final):
    tm = x_ref.shape[0]
    i = pl.program_id(0)
    n = pl.num_programs(0)
    slot = i % 2

    @pl.when(i == 0)
    def _():
        _start_row_gather(cur_ref, 2 * tm, y_ref, ybuf, 0, sem)

    @pl.when(i + 1 < n)
    def _():
        _start_row_gather(nxt_ref, 2 * tm, y_ref, ybuf, 1 - slot, sem)

    def compute(s):
        _wait_row_gather(2 * tm, y_ref, ybuf, s, sem)
        y0 = _rows_from_tiles(ybuf, s, 0, tm)
        y1 = _rows_from_tiles(ybuf, s, tm, tm)
        wt = wt_ref[...]
        xo = x_ref[...] + wt[:, 4:5] * y0 + wt[:, 5:6] * y1
        if final:
            xo = _rms(xo, fg_ref[...])
        o_ref[...] = xo

    for s in range(2):
        @pl.when(slot == s)
        def _(s=s):
            compute(s)


def _combine(dest, x2d, wt, fg, ybuf, *, final):
    t = x2d.shape[0]
    tm = TM_COMB
    nsteps = t // tm
    row = lambda i: (i, 0)
    return pl.pallas_call(
        functools.partial(_comb_kernel, final=final),
        out_shape=jax.ShapeDtypeStruct((t, D_MODEL), F32),
        grid=(nsteps,),
        in_specs=[
            pl.BlockSpec((1, 1, 2 * tm), lambda i: (i, 0, 0), memory_space=pltpu.SMEM),
            pl.BlockSpec((1, 1, 2 * tm), lambda i: (jnp.minimum(i + 1, nsteps - 1), 0, 0),
                         memory_space=pltpu.SMEM),
            pl.BlockSpec((tm, D_MODEL), row),
            pl.BlockSpec((tm, SUBLANES), row),
            pl.BlockSpec((1, D_MODEL), lambda i: (0, 0)),
            pl.BlockSpec(memory_space=pl.ANY),
        ],
        out_specs=pl.BlockSpec((tm, D_MODEL), row),
        scratch_shapes=[pltpu.VMEM((2, 2 * tm * CHUNKS, LANES), F32),
                        pltpu.SemaphoreType.DMA((2,))],
        compiler_params=pltpu.CompilerParams(
            dimension_semantics=("arbitrary",), vmem_limit_bytes=VMEM_LIMIT),
        name="combine",
    )(dest, dest, x2d, wt, fg, ybuf)


def _rope_tables():
    pos = jnp.arange(SEQ, dtype=F32)
    inv = ROPE_THETA ** (-jnp.arange(0, HEAD_DIM, 2, dtype=F32) / HEAD_DIM)
    ang1 = pos[:, None] * inv[None, :]
    rows = SEQ // GRID_W
    r = jnp.broadcast_to(jnp.arange(rows, dtype=F32)[:, None], (rows, GRID_W)).reshape(-1)
    c = jnp.broadcast_to(jnp.arange(GRID_W, dtype=F32)[None, :], (rows, GRID_W)).reshape(-1)
    axis_dim = HEAD_DIM // 2
    inv2 = ROPE_THETA ** (-jnp.arange(0, axis_dim, 2, dtype=F32) / axis_dim)
    ang2 = jnp.concatenate([r[:, None] * inv2, c[:, None] * inv2], axis=-1)

    def tables(ang):
        cs, sn = jnp.cos(ang), jnp.sin(ang)
        reps = LANES // HEAD_DIM
        return (jnp.tile(jnp.concatenate([cs, cs], axis=1), (1, reps)),
                jnp.tile(jnp.concatenate([-sn, sn], axis=1), (1, reps)))

    return tables(ang1) + tables(ang2)


def kernel(x, mem, mem_norm_g, w_mem_kv, norm1_g, w_in, lam_q1, lam_k1, lam_q2, lam_k2,
           subln_g, q_norm_g, k_norm_g, w_up_a, w_up_b, w_up_c, w_out, norm2_g,
           w_router_group, b_router_group, w_router_expert, b_router_expert,
           w_exp_gate, w_exp_up, w_exp_down, final_norm_g):
    batch, seq, d = x.shape
    depth = w_in.shape[0]
    assert (seq, d, mem.shape[1]) == (SEQ, D_MODEL, MEM_LEN)
    t = batch * seq
    ca, sa, cb, sb = _rope_tables()
    gidx = jnp.arange(512) // HEAD_DIM
    bd = (gidx[:, None] == gidx[None, :]).astype(BF16)
    ti = jnp.arange(TM_POST)
    tri = (ti[:, None] < ti[None, :]).astype(BF16)
    nblk = (t * TOP_K) // BM + N_EXPERTS

    mkv = _memkv(mem.reshape(batch * MEM_LEN, d), mem_norm_g.reshape(1, d), w_mem_kv.astype(BF16))
    x2d = x.reshape(t, d)
    for l in range(depth):
        wl = w_in[l]
        bk0, bk1 = wl[:, 2048:2112], wl[:, 2112:2176]
        bv0, bv1 = wl[:, 2176:2240], wl[:, 2240:2304]
        w = jnp.concatenate([wl[:, :2048], bk0, bk0, bk1, bk1, bv0, bv0, bv1, bv1, wl[:, 2304:]],
                            axis=1).astype(BF16)
        aq, ak, av, bq, bk, bv, cq, gate = _in_proj(
            x2d, norm1_g[l].reshape(1, d), w, ca, sa, cb, sb,
            jnp.tile(q_norm_g[l], 8).reshape(1, 512), jnp.tile(k_norm_g[l], 4).reshape(1, 256), bd)

        lam_init = 0.8 - 0.6 * math.exp(-0.3 * l)
        lamp = jnp.stack([lam_q1[l], lam_k1[l], lam_q2[l], lam_k2[l]]).astype(F32)
        oa = _attention(aq, ak, av, batch, diff=True, lamp=lamp,
                        gs=subln_g[l].reshape(1, LANES), lam_init=lam_init)
        ob = _attention(bq, bk, bv, batch, diff=False)

        wr = jnp.zeros((ROUTE_ROWS, d), F32)
        wr = wr.at[0:N_GROUPS].set(w_router_group[l].T).at[SUBLANES:].set(w_router_expert[l].T)
        wr_hi = wr.astype(BF16)
        wr_lo = (wr - wr_hi.astype(F32)).astype(BF16)
        br = jnp.zeros((ROUTE_ROWS, 1), F32)
        br = br.at[0:N_GROUPS, 0].set(b_router_group[l]).at[SUBLANES:, 0].set(b_router_expert[l])
        x2d, h2, route, cnt = _post(
            x2d, oa, ob, cq, gate, mkv, w_up_a[l].astype(BF16), w_up_b[l].astype(BF16),
            w_up_c[l].astype(BF16), w_out[l].astype(BF16), norm2_g[l].reshape(1, d),
            jnp.concatenate([wr_hi, wr_lo], axis=0), br, tri)

        counts = cnt[:, 0].astype(jnp.int32)
        padded = (counts + BM - 1) // BM * BM
        pends = jnp.cumsum(padded)
        pstarts = pends - padded
        nb_used = (pends[-1] // BM).reshape(1)
        block_e = jnp.minimum(
            jnp.searchsorted(pends, jnp.arange(nblk, dtype=jnp.int32) * BM, side='right'),
            N_EXPERTS - 1).astype(jnp.int32)
        experts = route[0:2].astype(jnp.int32)
        dest = pstarts[experts] + route[2:4].astype(jnp.int32)
        tok = jnp.broadcast_to(jnp.arange(t, dtype=jnp.int32)[None, :], (2, t))
        src = jnp.zeros((nblk * BM,), jnp.int32).at[dest.reshape(-1)].set(tok.reshape(-1))
        ybuf = _moe(block_e, nb_used, src.reshape(nblk, 1, BM), h2,
                    w_exp_gate[l], w_exp_up[l], w_exp_down[l])

        dest_blk = dest.reshape(2, t // TM_COMB, TM_COMB).transpose(1, 0, 2).reshape(
            t // TM_COMB, 1, 2 * TM_COMB)
        x2d = _combine(dest_blk, x2d, route.T, final_norm_g.reshape(1, d), ybuf,
                       final=(l == depth - 1))
    return x2d.reshape(batch, seq, d)
```

```python
import functools
import math

import jax
import jax.numpy as jnp
from jax import lax
from jax.experimental import pallas as pl
from jax.experimental.pallas import tpu as pltpu

F32 = jnp.float32
BF16 = jnp.bfloat16

D_MODEL = 1024
SEQ = 2048
MEM_LEN = 256
HEAD_DIM = 64
MX_HEAD_DIM = 128
BRANCH_W = 512
GRID_W = 64
ROPE_THETA = 10000.0
NORM_EPS = 1e-6
N_GROUPS = 4
EXPERTS_PER_GROUP = 8
N_EXPERTS = N_GROUPS * EXPERTS_PER_GROUP
TOP_K = 2
D_EXPERT = 512

LANES = 128
SUBLANES = 8
CHUNKS = D_MODEL // LANES

TM_IN = 256
TQ = 512
TK = 512
ONES_ROWS = 16
LOG2E = math.log2(math.e)
TM_POST = 256
BM = 256
TM_COMB = 256
ISSUE_UNROLL = 8
VMEM_LIMIT = 56 * 1024 * 1024

C_AQ, C_AK, C_AV, C_BQ, C_BK, C_BV, C_CQ, C_G, C_END = (
    0, 512, 1024, 1536, 2048, 2304, 2560, 3072, 6144)
ROUTE_ROWS = 40


def _rms(xf, g):
    ms = jnp.mean(xf * xf, axis=-1, keepdims=True)
    return xf * lax.rsqrt(ms + NORM_EPS) * g


def _dot(a, b):
    return jnp.dot(a, b, preferred_element_type=F32)


def _dot_nt(a, b):
    return lax.dot_general(a, b, (((1,), (1,)), ((), ())), preferred_element_type=F32)


def _memkv_kernel(m_ref, g_ref, w_ref, o_ref):
    h = _rms(m_ref[...], g_ref[...]).astype(BF16)
    o_ref[...] = _dot(h, w_ref[...]).astype(BF16)


def _memkv(mem2d, g, w):
    n = mem2d.shape[0]
    tm = 512
    return pl.pallas_call(
        _memkv_kernel,
        out_shape=jax.ShapeDtypeStruct((n, w.shape[1]), BF16),
        grid=(n // tm,),
        in_specs=[pl.BlockSpec((tm, D_MODEL), lambda i: (i, 0)),
                  pl.BlockSpec((1, D_MODEL), lambda i: (0, 0)),
                  pl.BlockSpec(w.shape, lambda i: (0, 0))],
        out_specs=pl.BlockSpec((tm, w.shape[1]), lambda i: (i, 0)),
        compiler_params=pltpu.CompilerParams(
            dimension_semantics=("arbitrary",), vmem_limit_bytes=VMEM_LIMIT),
        name="memkv",
    )(mem2d, g, w)


def _in_kernel(x_ref, g1_ref, w_ref, ca_ref, sa_ref, cb_ref, sb_ref, qg_ref, kg_ref, bd_ref,
               aq_ref, ak_ref, av_ref, bq_ref, bk_ref, bv_ref, cq_ref, gate_ref):
    tm = x_ref.shape[0]
    h = _rms(x_ref[...], g1_ref[...]).astype(BF16)
    lane = lax.broadcasted_iota(jnp.int32, (tm, LANES), 1)
    first_half = (lane & (HEAD_DIM // 2)) == 0

    def seg(lo, hi):
        return _dot(h, w_ref[:, lo:hi])

    def rope(p, c, s):
        sw = jnp.where(first_half, pltpu.roll(p, LANES - HEAD_DIM // 2, 1),
                       pltpu.roll(p, HEAD_DIM // 2, 1))
        return p * c + sw * s

    def group_norm(p, gain):
        n = p.shape[1]
        ss = _dot((p * p).astype(BF16), bd_ref[:n, :n])
        return p * lax.rsqrt(ss * (1.0 / HEAD_DIM) + NORM_EPS) * gain

    def rope_store(p, c_ref, s_ref, o_ref, scale):
        c = c_ref[...]
        s = s_ref[...]
        for j in range(p.shape[1] // LANES):
            sl = slice(j * LANES, (j + 1) * LANES)
            o_ref[:, sl] = (rope(p[:, sl], c, s) * scale).astype(BF16)

    q_scale = HEAD_DIM ** -0.5 * LOG2E
    rope_store(seg(C_AQ, C_AK), ca_ref, sa_ref, aq_ref, q_scale)
    rope_store(seg(C_AK, C_AV), ca_ref, sa_ref, ak_ref, 1.0)
    av_ref[...] = seg(C_AV, C_BQ).T.astype(BF16)
    rope_store(group_norm(seg(C_BQ, C_BK), qg_ref[...]), cb_ref, sb_ref, bq_ref, q_scale)
    rope_store(group_norm(seg(C_BK, C_BV), kg_ref[...]), cb_ref, sb_ref, bk_ref, 1.0)
    bv_ref[...] = seg(C_BV, C_CQ).T.astype(BF16)
    cq_ref[...] = (seg(C_CQ, C_G) * (MX_HEAD_DIM ** -0.5)).astype(BF16)
    for j in range((C_END - C_G) // 512):
        lo = C_G + j * 512
        z = seg(lo, lo + 512)
        gate_ref[:, j * 512:(j + 1) * 512] = (1.0 / (1.0 + jnp.exp(-z))).astype(BF16)


def _in_proj(x2d, g1, w, ca, sa, cb, sb, qg, kg, bd):
    t = x2d.shape[0]
    tm = TM_IN
    nrb = SEQ // tm
    row = lambda i: (i, 0)
    const = lambda i: (0, 0)
    tab = lambda i: (i % nrb, 0)
    outs = ((512, False), (512, False), (512, True), (512, False), (256, False), (256, True),
            (512, False), (3072, False))
    col = lambda i: (0, i)
    return pl.pallas_call(
        _in_kernel,
        out_shape=[jax.ShapeDtypeStruct((n, t) if tr else (t, n), BF16) for n, tr in outs],
        grid=(t // tm,),
        in_specs=[pl.BlockSpec((tm, D_MODEL), row),
                  pl.BlockSpec((1, D_MODEL), const),
                  pl.BlockSpec(w.shape, const),
                  pl.BlockSpec((tm, LANES), tab), pl.BlockSpec((tm, LANES), tab),
                  pl.BlockSpec((tm, LANES), tab), pl.BlockSpec((tm, LANES), tab),
                  pl.BlockSpec((1, 512), const), pl.BlockSpec((1, 256), const),
                  pl.BlockSpec((512, 512), const)],
        out_specs=[pl.BlockSpec((n, tm), col) if tr else pl.BlockSpec((tm, n), row)
                   for n, tr in outs],
        compiler_params=pltpu.CompilerParams(
            dimension_semantics=("arbitrary",), vmem_limit_bytes=VMEM_LIMIT),
        name="in_proj",
    )(x2d, g1, w, ca, sa, cb, sb, qg, kg, bd)


def _attn_kernel(*refs, diff, post_scale, lam_init):
    if diff:
        lamp_ref, gs_ref, q_ref, k_ref, vt_ref, o_ref = refs
    else:
        q_ref, k_ref, vt_ref, o_ref = refs
    tq = q_ref.shape[0]
    q = q_ref[...]
    lane = lax.broadcasted_iota(jnp.int32, (tq, LANES), 1)
    lo = lane < HEAD_DIM
    zero = jnp.zeros_like(q)
    qs = jnp.concatenate([jnp.where(lo, q, zero), jnp.where(lo, zero, q)], axis=0)
    cols = 2 * tq
    ones = jnp.ones((ONES_ROWS, TK), BF16)
    m = jnp.full((1, cols), -jnp.inf, F32)
    acc = jnp.zeros((LANES + ONES_ROWS, cols), F32)
    nchunks = SEQ // TK

    def scores(j):
        return _dot_nt(k_ref[j * TK:(j + 1) * TK, :], qs)

    st_next = scores(0)
    for j in range(nchunks):
        st = st_next
        if j + 1 < nchunks:
            st_next = scores(j + 1)
        vtj = jnp.concatenate([vt_ref[:, j * TK:(j + 1) * TK], ones], axis=0)
        m_new = jnp.maximum(m, jnp.max(st, axis=0, keepdims=True))
        alpha = jnp.exp2(m - m_new)
        e = jnp.exp2(st - m_new).astype(BF16)
        acc = alpha * acc + _dot(vtj, e)
        m = m_new
    o = acc[:LANES] / acc[LANES:LANES + 1]
    if diff:
        lp = lamp_ref[...]
        lam = (jnp.exp(jnp.sum(lp[0:1] * lp[1:2], axis=-1, keepdims=True))
               - jnp.exp(jnp.sum(lp[2:3] * lp[3:4], axis=-1, keepdims=True)) + lam_init)
        d = o[:, :tq] - lam * o[:, tq:]
        ms = jnp.mean(d * d, axis=0, keepdims=True)
        out_t = d * lax.rsqrt(ms + NORM_EPS) * gs_ref[...] * post_scale
    else:
        row = lax.broadcasted_iota(jnp.int32, (LANES, tq), 0)
        out_t = jnp.where(row < HEAD_DIM, o[:, :tq], o[:, tq:])
    o_ref[...] = out_t.T.astype(BF16)


def _attention(q, k, vt, batch, *, diff, lamp=None, gs=None, lam_init=0.0):
    t = q.shape[0]
    nq = SEQ // TQ
    nblk = q.shape[1] // LANES
    kv_per = nblk // (k.shape[1] // LANES)
    qmap = lambda b, h, i: (b * nq + i, h)
    in_specs = [pl.BlockSpec((TQ, LANES), qmap),
                pl.BlockSpec((SEQ, LANES), lambda b, h, i: (b, h // kv_per)),
                pl.BlockSpec((LANES, SEQ), lambda b, h, i: (h // kv_per, b))]
    args = [q, k, vt]
    if diff:
        const = lambda b, h, i: (0, 0)
        in_specs = [pl.BlockSpec((4, HEAD_DIM), const), pl.BlockSpec((LANES, 1), const)] + in_specs
        args = [lamp, gs] + args
    return pl.pallas_call(
        functools.partial(_attn_kernel, diff=diff, post_scale=1.0 - lam_init, lam_init=lam_init),
        out_shape=jax.ShapeDtypeStruct((t, q.shape[1]), BF16),
        grid=(batch, nblk, nq),
        in_specs=in_specs,
        out_specs=pl.BlockSpec((TQ, LANES), qmap),
        compiler_params=pltpu.CompilerParams(
            dimension_semantics=("arbitrary", "arbitrary", "arbitrary"),
            vmem_limit_bytes=VMEM_LIMIT),
        name="attn_diff" if diff else "attn_gqa",
    )(*args)


def _post_kernel(x_ref, oa_ref, ob_ref, cq_ref, gate_ref, mkv_ref, wa_ref, wb_ref, wc_ref,
                 wo_ref, g2_ref, wr_ref, br_ref, tri_ref,
                 xo_ref, h2_ref, route_ref, cnt_ref, carry_ref):
    tm = x_ref.shape[0]
    i = pl.program_id(0)

    @pl.when(i == 0)
    def _():
        carry_ref[...] = jnp.zeros_like(carry_ref)

    heads = []
    for hd in range(BRANCH_W // MX_HEAD_DIM):
        sl = slice(hd * MX_HEAD_DIM, (hd + 1) * MX_HEAD_DIM)
        sv = slice(BRANCH_W + hd * MX_HEAD_DIM, BRANCH_W + (hd + 1) * MX_HEAD_DIM)
        s = _dot_nt(cq_ref[:, sl], mkv_ref[:, sl])
        e = jnp.exp(s - jnp.max(s, axis=-1, keepdims=True))
        den = jnp.sum(e, axis=-1, keepdims=True)
        heads.append((_dot(e.astype(BF16), mkv_ref[:, sv]) / den).astype(BF16))
    oc = jnp.concatenate(heads, axis=1)

    ya = _dot(oa_ref[...], wa_ref[...])
    yb = _dot(ob_ref[...], wb_ref[...])
    yc = _dot(oc, wc_ref[...])
    merged = (gate_ref[:, 0:D_MODEL].astype(F32) * ya
              + gate_ref[:, D_MODEL:2 * D_MODEL].astype(F32) * yb
              + gate_ref[:, 2 * D_MODEL:3 * D_MODEL].astype(F32) * yc)
    xn = x_ref[...] + _dot(merged.astype(BF16), wo_ref[...])
    xo_ref[...] = xn
    h2 = _rms(xn, g2_ref[...])
    for c in range(CHUNKS):
        h2_ref[pl.ds(c, tm, stride=CHUNKS), :] = h2[:, c * LANES:(c + 1) * LANES]

    h_hi = h2.astype(BF16)
    h_lo = (h2 - h_hi.astype(F32)).astype(BF16)
    l2 = _dot_nt(wr_ref[...], h_hi)
    logits = (l2[:ROUTE_ROWS] + l2[ROUTE_ROWS:] + _dot_nt(wr_ref[:ROUTE_ROWS, :], h_lo)
              + br_ref[...])

    neg = -jnp.inf
    r8 = lax.broadcasted_iota(jnp.int32, (SUBLANES, tm), 0)
    r32 = lax.broadcasted_iota(jnp.int32, (N_EXPERTS, tm), 0)
    gl = jnp.where(r8 < N_GROUPS, logits[0:SUBLANES], neg)
    gmax = jnp.max(gl, axis=0, keepdims=True)
    gidx = jnp.min(jnp.where(gl == gmax, r8, SUBLANES), axis=0, keepdims=True)
    gp = 1.0 / jnp.sum(jnp.exp(gl - gmax), axis=0, keepdims=True)
    el = jnp.where((r32 // EXPERTS_PER_GROUP) == gidx, logits[SUBLANES:ROUTE_ROWS], neg)
    m1 = jnp.max(el, axis=0, keepdims=True)
    i1 = jnp.min(jnp.where(el == m1, r32, N_EXPERTS), axis=0, keepdims=True)
    el2 = jnp.where(r32 == i1, neg, el)
    m2 = jnp.max(el2, axis=0, keepdims=True)
    i2 = jnp.min(jnp.where(el2 == m2, r32, N_EXPERTS), axis=0, keepdims=True)
    d = jnp.exp(m2 - m1)
    w1 = gp / (1.0 + d)
    w2 = gp * d / (1.0 + d)

    hit1 = r32 == i1
    hit2 = r32 == i2
    oh = jnp.where(hit1 | hit2, 1.0, 0.0)
    before = _dot(oh.astype(BF16), tri_ref[...]) + carry_ref[...]
    rank1 = jnp.sum(jnp.where(hit1, before, 0.0), axis=0, keepdims=True)
    rank2 = jnp.sum(jnp.where(hit2, before, 0.0), axis=0, keepdims=True)
    carry_ref[...] = carry_ref[...] + jnp.sum(oh, axis=1, keepdims=True)
    zrow = jnp.zeros_like(w1)
    route_ref[...] = jnp.concatenate(
        [i1.astype(F32), i2.astype(F32), rank1, rank2, w1, w2, zrow, zrow], axis=0)
    cnt_ref[...] = jnp.broadcast_to(carry_ref[...], cnt_ref.shape)


def _post(x2d, oa, ob, cq, gate, mkv, wa, wb, wc, wo, g2, wr, br, tri):
    t = x2d.shape[0]
    tm = TM_POST
    nrb = SEQ // tm
    row = lambda i: (i, 0)
    const = lambda i: (0, 0)
    return pl.pallas_call(
        _post_kernel,
        out_shape=[jax.ShapeDtypeStruct((t, D_MODEL), F32),
                   jax.ShapeDtypeStruct((t * CHUNKS, LANES), F32),
                   jax.ShapeDtypeStruct((SUBLANES, t), F32),
                   jax.ShapeDtypeStruct((N_EXPERTS, LANES), F32)],
        grid=(t // tm,),
        in_specs=[pl.BlockSpec((tm, D_MODEL), row),
                  pl.BlockSpec((tm, BRANCH_W), row),
                  pl.BlockSpec((tm, BRANCH_W), row),
                  pl.BlockSpec((tm, BRANCH_W), row),
                  pl.BlockSpec((tm, 3 * D_MODEL), row),
                  pl.BlockSpec((MEM_LEN, 2 * BRANCH_W), lambda i: (i // nrb, 0)),
                  pl.BlockSpec(wa.shape, const), pl.BlockSpec(wb.shape, const),
                  pl.BlockSpec(wc.shape, const), pl.BlockSpec(wo.shape, const),
                  pl.BlockSpec((1, D_MODEL), const),
                  pl.BlockSpec(wr.shape, const), pl.BlockSpec(br.shape, const),
                  pl.BlockSpec(tri.shape, const)],
        out_specs=[pl.BlockSpec((tm, D_MODEL), row),
                   pl.BlockSpec((tm * CHUNKS, LANES), row),
                   pl.BlockSpec((SUBLANES, tm), lambda i: (0, i)),
                   pl.BlockSpec((N_EXPERTS, LANES), const)],
        scratch_shapes=[pltpu.VMEM((N_EXPERTS, 1), F32)],
        compiler_params=pltpu.CompilerParams(
            dimension_semantics=("arbitrary",), vmem_limit_bytes=VMEM_LIMIT),
        name="post",
    )(x2d, oa, ob, cq, gate, mkv, wa, wb, wc, wo, g2, wr, br, tri)


def _start_row_gather(idx_ref, n, src_ref, dst_ref, slot, sem):
    def body(i, carry):
        for u in range(ISSUE_UNROLL):
            r = i * ISSUE_UNROLL + u
            tok = idx_ref[0, 0, r]
            pltpu.make_async_copy(
                src_ref.at[pl.ds(pl.multiple_of(tok * CHUNKS, CHUNKS), CHUNKS)],
                dst_ref.at[slot, pl.ds(pl.multiple_of(r * CHUNKS, CHUNKS), CHUNKS)],
                sem.at[slot]).start()
        return carry
    lax.fori_loop(0, n // ISSUE_UNROLL, body, 0)


def _wait_row_gather(n, src_ref, dst_ref, slot, sem):
    pltpu.make_async_copy(src_ref.at[pl.ds(0, n * CHUNKS)], dst_ref.at[slot], sem.at[slot]).wait()


def _rows_from_tiles(buf_ref, slot, first, n):
    return jnp.concatenate(
        [buf_ref[slot, pl.ds(first * CHUNKS + c, n, stride=CHUNKS), :] for c in range(CHUNKS)],
        axis=1)


def _moe_kernel(be_ref, nb_ref, cur_ref, nxt_ref, h2_ref, wg_ref, wu_ref, wd_ref, y_ref,
                xbuf, sem, wgb, wub, wdb):
    b = pl.program_id(0)
    nb = nb_ref[0]
    slot = b % 2

    @pl.when(b == 0)
    def _():
        _start_row_gather(cur_ref, BM, h2_ref, xbuf, 0, sem)

    @pl.when(b + 1 < nb)
    def _():
        _start_row_gather(nxt_ref, BM, h2_ref, xbuf, 1 - slot, sem)

    @pl.when((b == 0) | (be_ref[b] != be_ref[jnp.maximum(b - 1, 0)]))
    def _():
        wgb[...] = wg_ref[0, 0].astype(BF16)
        wub[...] = wu_ref[0, 0].astype(BF16)
        wdb[...] = wd_ref[0, 0].astype(BF16)

    def compute(s):
        _wait_row_gather(BM, h2_ref, xbuf, s, sem)
        xb = _rows_from_tiles(xbuf, s, 0, BM).astype(BF16)
        hg = _dot(xb, wgb[...])
        hu = _dot(xb, wub[...])
        hid = (hg / (1.0 + jnp.exp(-hg)) * hu).astype(BF16)
        y = _dot(hid, wdb[...])
        for c in range(CHUNKS):
            y_ref[pl.ds(c, BM, stride=CHUNKS), :] = y[:, c * LANES:(c + 1) * LANES]

    for s in range(2):
        @pl.when((b < nb) & (slot == s))
        def _(s=s):
            compute(s)

    @pl.when(b >= nb)
    def _():
        y_ref[...] = jnp.zeros_like(y_ref)


def _moe(block_e, nb_used, src, h2, wg, wu, wd, layer):
    nblk = src.shape[0]
    wmap = lambda b, be, nb: (layer, be[b], 0, 0)
    return pl.pallas_call(
        _moe_kernel,
        out_shape=jax.ShapeDtypeStruct((nblk * BM * CHUNKS, LANES), F32),
        grid_spec=pltpu.PrefetchScalarGridSpec(
            num_scalar_prefetch=2,
            grid=(nblk,),
            in_specs=[
                pl.BlockSpec((1, 1, BM), lambda b, be, nb: (b, 0, 0), memory_space=pltpu.SMEM),
                pl.BlockSpec((1, 1, BM), lambda b, be, nb: (jnp.minimum(b + 1, nblk - 1), 0, 0),
                             memory_space=pltpu.SMEM),
                pl.BlockSpec(memory_space=pl.ANY),
                pl.BlockSpec((1, 1, D_MODEL, D_EXPERT), wmap),
                pl.BlockSpec((1, 1, D_MODEL, D_EXPERT), wmap),
                pl.BlockSpec((1, 1, D_EXPERT, D_MODEL), wmap),
            ],
            out_specs=pl.BlockSpec((BM * CHUNKS, LANES), lambda b, be, nb: (b, 0)),
            scratch_shapes=[pltpu.VMEM((2, BM * CHUNKS, LANES), F32),
                            pltpu.SemaphoreType.DMA((2,)),
                            pltpu.VMEM((D_MODEL, D_EXPERT), BF16),
                            pltpu.VMEM((D_MODEL, D_EXPERT), BF16),
                            pltpu.VMEM((D_EXPERT, D_MODEL), BF16)]),
        compiler_params=pltpu.CompilerParams(
            dimension_semantics=("arbitrary",), vmem_limit_bytes=VMEM_LIMIT),
        name="experts",
    )(block_e, nb_used, src, src, h2, wg, wu, wd)


def _comb_kernel(cur_ref, nxt_ref, x_ref, wt_ref, fg_ref, y_ref, o_ref, ybuf, sem, *, final):
    tm = x_ref.shape[0]
    i = pl.program_id(0)
    n = pl.num_programs(0)
    slot = i % 2

    @pl.when(i == 0)
    def _():
        _start_row_gather(cur_ref, 2 * tm, y_ref, ybuf, 0, sem)

    @pl.when(i + 1 < n)
    def _():
        _start_row_gather(nxt_ref, 2 * tm, y_ref, ybuf, 1 - slot, sem)

    def compute(s):
        _wait_row_gather(2 * tm, y_ref, ybuf, s, sem)
        y0 = _rows_from_tiles(ybuf, s, 0, tm)
        y1 = _rows_from_tiles(ybuf, s, tm, tm)
        wt = wt_ref[...]
        xo = x_ref[...] + wt[:, 4:5] * y0 + wt[:, 5:6] * y1
        if final:
            xo = _rms(xo, fg_ref[...])
        o_ref[...] = xo

    for s in range(2):
        @pl.when(slot == s)
        def _(s=s):
            compute(s)


def _combine(dest, x2d, wt, fg, ybuf, *, final):
    t = x2d.shape[0]
    tm = TM_COMB
    nsteps = t // tm
    row = lambda i: (i, 0)
    return pl.pallas_call(
        functools.partial(_comb_kernel, final=final),
        out_shape=jax.ShapeDtypeStruct((t, D_MODEL), F32),
        grid=(nsteps,),
        in_specs=[
            pl.BlockSpec((1, 1, 2 * tm), lambda i: (i, 0, 0), memory_space=pltpu.SMEM),
            pl.BlockSpec((1, 1, 2 * tm), lambda i: (jnp.minimum(i + 1, nsteps - 1), 0, 0),
                         memory_space=pltpu.SMEM),
            pl.BlockSpec((tm, D_MODEL), row),
            pl.BlockSpec((tm, SUBLANES), row),
            pl.BlockSpec((1, D_MODEL), lambda i: (0, 0)),
            pl.BlockSpec(memory_space=pl.ANY),
        ],
        out_specs=pl.BlockSpec((tm, D_MODEL), row),
        scratch_shapes=[pltpu.VMEM((2, 2 * tm * CHUNKS, LANES), F32),
                        pltpu.SemaphoreType.DMA((2,))],
        compiler_params=pltpu.CompilerParams(
            dimension_semantics=("arbitrary",), vmem_limit_bytes=VMEM_LIMIT),
        name="combine",
    )(dest, dest, x2d, wt, fg, ybuf)


def _rope_tables():
    pos = jnp.arange(SEQ, dtype=F32)
    inv = ROPE_THETA ** (-jnp.arange(0, HEAD_DIM, 2, dtype=F32) / HEAD_DIM)
    ang1 = pos[:, None] * inv[None, :]
    rows = SEQ // GRID_W
    r = jnp.broadcast_to(jnp.arange(rows, dtype=F32)[:, None], (rows, GRID_W)).reshape(-1)
    c = jnp.broadcast_to(jnp.arange(GRID_W, dtype=F32)[None, :], (rows, GRID_W)).reshape(-1)
    axis_dim = HEAD_DIM // 2
    inv2 = ROPE_THETA ** (-jnp.arange(0, axis_dim, 2, dtype=F32) / axis_dim)
    ang2 = jnp.concatenate([r[:, None] * inv2, c[:, None] * inv2], axis=-1)

    def tables(ang):
        cs, sn = jnp.cos(ang), jnp.sin(ang)
        reps = LANES // HEAD_DIM
        return (jnp.tile(jnp.concatenate([cs, cs], axis=1), (1, reps)),
                jnp.tile(jnp.concatenate([-sn, sn], axis=1), (1, reps)))

    return tables(ang1) + tables(ang2)


def kernel(x, mem, mem_norm_g, w_mem_kv, norm1_g, w_in, lam_q1, lam_k1, lam_q2, lam_k2,
           subln_g, q_norm_g, k_norm_g, w_up_a, w_up_b, w_up_c, w_out, norm2_g,
           w_router_group, b_router_group, w_router_expert, b_router_expert,
           w_exp_gate, w_exp_up, w_exp_down, final_norm_g):
    batch, seq, d = x.shape
    depth = w_in.shape[0]
    assert (seq, d, mem.shape[1]) == (SEQ, D_MODEL, MEM_LEN)
    t = batch * seq
    ca, sa, cb, sb = _rope_tables()
    gidx = jnp.arange(512) // HEAD_DIM
    bd = (gidx[:, None] == gidx[None, :]).astype(BF16)
    ti = jnp.arange(TM_POST)
    tri = (ti[:, None] < ti[None, :]).astype(BF16)
    nblk = (t * TOP_K) // BM + N_EXPERTS

    mkv = _memkv(mem.reshape(batch * MEM_LEN, d), mem_norm_g.reshape(1, d), w_mem_kv.astype(BF16))
    x2d = x.reshape(t, d)
    for l in range(depth):
        wl = w_in[l]
        bk0, bk1 = wl[:, 2048:2112], wl[:, 2112:2176]
        bv0, bv1 = wl[:, 2176:2240], wl[:, 2240:2304]
        w = jnp.concatenate([wl[:, :2048], bk0, bk0, bk1, bk1, bv0, bv0, bv1, bv1, wl[:, 2304:]],
                            axis=1).astype(BF16)
        aq, ak, av, bq, bk, bv, cq, gate = _in_proj(
            x2d, norm1_g[l].reshape(1, d), w, ca, sa, cb, sb,
            jnp.tile(q_norm_g[l], 8).reshape(1, 512), jnp.tile(k_norm_g[l], 4).reshape(1, 256), bd)

        lam_init = 0.8 - 0.6 * math.exp(-0.3 * l)
        lamp = jnp.stack([lam_q1[l], lam_k1[l], lam_q2[l], lam_k2[l]]).astype(F32)
        oa = _attention(aq, ak, av, batch, diff=True, lamp=lamp,
                        gs=subln_g[l].reshape(LANES, 1), lam_init=lam_init)
        ob = _attention(bq, bk, bv, batch, diff=False)

        wr = jnp.zeros((ROUTE_ROWS, d), F32)
        wr = wr.at[0:N_GROUPS].set(w_router_group[l].T).at[SUBLANES:].set(w_router_expert[l].T)
        wr_hi = wr.astype(BF16)
        wr_lo = (wr - wr_hi.astype(F32)).astype(BF16)
        br = jnp.zeros((ROUTE_ROWS, 1), F32)
        br = br.at[0:N_GROUPS, 0].set(b_router_group[l]).at[SUBLANES:, 0].set(b_router_expert[l])
        x2d, h2, route, cnt = _post(
            x2d, oa, ob, cq, gate, mkv, w_up_a[l].astype(BF16), w_up_b[l].astype(BF16),
            w_up_c[l].astype(BF16), w_out[l].astype(BF16), norm2_g[l].reshape(1, d),
            jnp.concatenate([wr_hi, wr_lo], axis=0), br, tri)

        counts = cnt[:, 0].astype(jnp.int32)
        padded = (counts + BM - 1) // BM * BM
        pends = jnp.cumsum(padded)
        pstarts = pends - padded
        nb_used = (pends[-1] // BM).reshape(1)
        blk_start = jnp.arange(nblk, dtype=jnp.int32) * BM
        block_e = jnp.minimum(
            jnp.sum((pends[None, :] <= blk_start[:, None]).astype(jnp.int32), axis=1),
            N_EXPERTS - 1)
        experts = route[0:2].astype(jnp.int32)
        hit = experts[:, :, None] == jnp.arange(N_EXPERTS, dtype=jnp.int32)
        dest = (jnp.sum(jnp.where(hit, pstarts, 0), axis=-1)
                + route[2:4].astype(jnp.int32))
        tok = jnp.broadcast_to(jnp.arange(t, dtype=jnp.int32)[None, :], (2, t))
        src = jnp.zeros((nblk * BM,), jnp.int32).at[dest.reshape(-1)].set(
            tok.reshape(-1), unique_indices=True, mode='promise_in_bounds')
        ybuf = _moe(block_e, nb_used, src.reshape(nblk, 1, BM), h2,
                    w_exp_gate, w_exp_up, w_exp_down, l)

        dest_blk = dest.reshape(2, t // TM_COMB, TM_COMB).transpose(1, 0, 2).reshape(
            t // TM_COMB, 1, 2 * TM_COMB)
        x2d = _combine(dest_blk, x2d, route.T, final_norm_g.reshape(1, d), ybuf,
                       final=(l == depth - 1))
    return x2d.reshape(batch, seq, d)
```

```python
import functools
import math

import jax
import jax.numpy as jnp
from jax import lax
from jax.experimental import pallas as pl
from jax.experimental.pallas import tpu as pltpu

F32 = jnp.float32
BF16 = jnp.bfloat16

D_MODEL = 1024
SEQ = 2048
MEM_LEN = 256
HEAD_DIM = 64
MX_HEAD_DIM = 128
BRANCH_W = 512
GRID_W = 64
ROPE_THETA = 10000.0
NORM_EPS = 1e-6
N_GROUPS = 4
EXPERTS_PER_GROUP = 8
N_EXPERTS = N_GROUPS * EXPERTS_PER_GROUP
TOP_K = 2
D_EXPERT = 512

LANES = 128
SUBLANES = 8
CHUNKS = D_MODEL // LANES

TM_IN = 256
TQ = 512
TK = 512
ONES_ROWS = 16
LOG2E = math.log2(math.e)
TM_POST = 256
BM = 256
TM_COMB = 256
TD = 1024
ISSUE_UNROLL = 8
VMEM_LIMIT = 56 * 1024 * 1024

C_AQ, C_AK, C_AV, C_BQ, C_BK, C_BV, C_CQ, C_G, C_END = (
    0, 512, 1024, 1536, 2048, 2304, 2560, 3072, 6144)
ROUTE_ROWS = 40


def _rms(xf, g):
    ms = jnp.mean(xf * xf, axis=-1, keepdims=True)
    return xf * lax.rsqrt(ms + NORM_EPS) * g


def _dot(a, b):
    return jnp.dot(a, b, preferred_element_type=F32)


def _dot_nt(a, b):
    return lax.dot_general(a, b, (((1,), (1,)), ((), ())), preferred_element_type=F32)


def _memkv_kernel(m_ref, g_ref, w_ref, o_ref):
    h = _rms(m_ref[...], g_ref[...]).astype(BF16)
    o_ref[...] = _dot(h, w_ref[...]).astype(BF16)


def _memkv(mem2d, g, w):
    n = mem2d.shape[0]
    tm = 512
    return pl.pallas_call(
        _memkv_kernel,
        out_shape=jax.ShapeDtypeStruct((n, w.shape[1]), BF16),
        grid=(n // tm,),
        in_specs=[pl.BlockSpec((tm, D_MODEL), lambda i: (i, 0)),
                  pl.BlockSpec((1, D_MODEL), lambda i: (0, 0)),
                  pl.BlockSpec(w.shape, lambda i: (0, 0))],
        out_specs=pl.BlockSpec((tm, w.shape[1]), lambda i: (i, 0)),
        compiler_params=pltpu.CompilerParams(
            dimension_semantics=("arbitrary",), vmem_limit_bytes=VMEM_LIMIT),
        name="memkv",
    )(mem2d, g, w)


def _in_kernel(x_ref, g1_ref, w_ref, ca_ref, sa_ref, cb_ref, sb_ref, qg_ref, kg_ref, bd_ref,
               aq_ref, ak_ref, av_ref, bq_ref, bk_ref, bv_ref, cq_ref, gate_ref):
    tm = x_ref.shape[0]
    h = _rms(x_ref[...], g1_ref[...]).astype(BF16)
    lane = lax.broadcasted_iota(jnp.int32, (tm, LANES), 1)
    first_half = (lane & (HEAD_DIM // 2)) == 0

    def seg(lo, hi):
        return _dot(h, w_ref[:, lo:hi])

    def rope(p, c, s):
        sw = jnp.where(first_half, pltpu.roll(p, LANES - HEAD_DIM // 2, 1),
                       pltpu.roll(p, HEAD_DIM // 2, 1))
        return p * c + sw * s

    def group_norm(p, gain):
        n = p.shape[1]
        ss = _dot((p * p).astype(BF16), bd_ref[:n, :n])
        return p * lax.rsqrt(ss * (1.0 / HEAD_DIM) + NORM_EPS) * gain

    def rope_store(p, c_ref, s_ref, o_ref, scale):
        c = c_ref[...]
        s = s_ref[...]
        for j in range(p.shape[1] // LANES):
            sl = slice(j * LANES, (j + 1) * LANES)
            o_ref[:, sl] = (rope(p[:, sl], c, s) * scale).astype(BF16)

    q_scale = HEAD_DIM ** -0.5 * LOG2E
    rope_store(seg(C_AQ, C_AK), ca_ref, sa_ref, aq_ref, q_scale)
    rope_store(seg(C_AK, C_AV), ca_ref, sa_ref, ak_ref, 1.0)
    av_ref[...] = seg(C_AV, C_BQ).T.astype(BF16)
    rope_store(group_norm(seg(C_BQ, C_BK), qg_ref[...]), cb_ref, sb_ref, bq_ref, q_scale)
    rope_store(group_norm(seg(C_BK, C_BV), kg_ref[...]), cb_ref, sb_ref, bk_ref, 1.0)
    bv_ref[...] = seg(C_BV, C_CQ).T.astype(BF16)
    cq_ref[...] = (seg(C_CQ, C_G) * (MX_HEAD_DIM ** -0.5)).astype(BF16)
    for j in range((C_END - C_G) // 512):
        lo = C_G + j * 512
        z = seg(lo, lo + 512)
        gate_ref[:, j * 512:(j + 1) * 512] = (1.0 / (1.0 + jnp.exp(-z))).astype(BF16)


def _in_proj(x2d, g1, w, ca, sa, cb, sb, qg, kg, bd):
    t = x2d.shape[0]
    tm = TM_IN
    nrb = SEQ // tm
    row = lambda i: (i, 0)
    const = lambda i: (0, 0)
    tab = lambda i: (i % nrb, 0)
    outs = ((512, False), (512, False), (512, True), (512, False), (256, False), (256, True),
            (512, False), (3072, False))
    col = lambda i: (0, i)
    return pl.pallas_call(
        _in_kernel,
        out_shape=[jax.ShapeDtypeStruct((n, t) if tr else (t, n), BF16) for n, tr in outs],
        grid=(t // tm,),
        in_specs=[pl.BlockSpec((tm, D_MODEL), row),
                  pl.BlockSpec((1, D_MODEL), const),
                  pl.BlockSpec(w.shape, const),
                  pl.BlockSpec((tm, LANES), tab), pl.BlockSpec((tm, LANES), tab),
                  pl.BlockSpec((tm, LANES), tab), pl.BlockSpec((tm, LANES), tab),
                  pl.BlockSpec((1, 512), const), pl.BlockSpec((1, 256), const),
                  pl.BlockSpec((512, 512), const)],
        out_specs=[pl.BlockSpec((n, tm), col) if tr else pl.BlockSpec((tm, n), row)
                   for n, tr in outs],
        compiler_params=pltpu.CompilerParams(
            dimension_semantics=("arbitrary",), vmem_limit_bytes=VMEM_LIMIT),
        name="in_proj",
    )(x2d, g1, w, ca, sa, cb, sb, qg, kg, bd)


def _attn_kernel(*refs, diff, post_scale, lam_init):
    if diff:
        lamp_ref, gs_ref, q_ref, k_ref, vt_ref, o_ref = refs
    else:
        q_ref, k_ref, vt_ref, o_ref = refs
    tq = q_ref.shape[0]
    q = q_ref[...]
    lane = lax.broadcasted_iota(jnp.int32, (tq, LANES), 1)
    lo = lane < HEAD_DIM
    zero = jnp.zeros_like(q)
    qs = jnp.concatenate([jnp.where(lo, q, zero), jnp.where(lo, zero, q)], axis=0)
    cols = 2 * tq
    ones = jnp.ones((ONES_ROWS, TK), BF16)
    m = jnp.full((1, cols), -jnp.inf, F32)
    acc = jnp.zeros((LANES + ONES_ROWS, cols), F32)
    nchunks = SEQ // TK

    def scores(j):
        return _dot_nt(k_ref[j * TK:(j + 1) * TK, :], qs)

    st_next = scores(0)
    for j in range(nchunks):
        st = st_next
        if j + 1 < nchunks:
            st_next = scores(j + 1)
        vtj = jnp.concatenate([vt_ref[:, j * TK:(j + 1) * TK], ones], axis=0)
        m_new = jnp.maximum(m, jnp.max(st, axis=0, keepdims=True))
        alpha = jnp.exp2(m - m_new)
        e = jnp.exp2(st - m_new).astype(BF16)
        acc = alpha * acc + _dot(vtj, e)
        m = m_new
    o = acc[:LANES] / acc[LANES:LANES + 1]
    if diff:
        lp = lamp_ref[...]
        lam = (jnp.exp(jnp.sum(lp[0:1] * lp[1:2], axis=-1, keepdims=True))
               - jnp.exp(jnp.sum(lp[2:3] * lp[3:4], axis=-1, keepdims=True)) + lam_init)
        d = o[:, :tq] - lam * o[:, tq:]
        ms = jnp.mean(d * d, axis=0, keepdims=True)
        out_t = d * lax.rsqrt(ms + NORM_EPS) * gs_ref[...] * post_scale
    else:
        row = lax.broadcasted_iota(jnp.int32, (LANES, tq), 0)
        out_t = jnp.where(row < HEAD_DIM, o[:, :tq], o[:, tq:])
    o_ref[...] = out_t.T.astype(BF16)


def _attention(q, k, vt, batch, *, diff, lamp=None, gs=None, lam_init=0.0):
    t = q.shape[0]
    nq = SEQ // TQ
    nblk = q.shape[1] // LANES
    kv_per = nblk // (k.shape[1] // LANES)
    qmap = lambda b, h, i: (b * nq + i, h)
    in_specs = [pl.BlockSpec((TQ, LANES), qmap),
                pl.BlockSpec((SEQ, LANES), lambda b, h, i: (b, h // kv_per)),
                pl.BlockSpec((LANES, SEQ), lambda b, h, i: (h // kv_per, b))]
    args = [q, k, vt]
    if diff:
        const = lambda b, h, i: (0, 0)
        in_specs = [pl.BlockSpec((4, HEAD_DIM), const), pl.BlockSpec((LANES, 1), const)] + in_specs
        args = [lamp, gs] + args
    return pl.pallas_call(
        functools.partial(_attn_kernel, diff=diff, post_scale=1.0 - lam_init, lam_init=lam_init),
        out_shape=jax.ShapeDtypeStruct((t, q.shape[1]), BF16),
        grid=(batch, nblk, nq),
        in_specs=in_specs,
        out_specs=pl.BlockSpec((TQ, LANES), qmap),
        compiler_params=pltpu.CompilerParams(
            dimension_semantics=("arbitrary", "arbitrary", "arbitrary"),
            vmem_limit_bytes=VMEM_LIMIT),
        name="attn_diff" if diff else "attn_gqa",
    )(*args)


def _post_kernel(x_ref, oa_ref, ob_ref, cq_ref, gate_ref, mkv_ref, wa_ref, wb_ref, wc_ref,
                 wo_ref, g2_ref, wr_ref, br_ref, tri_ref,
                 xo_ref, h2_ref, route_ref, cnt_ref, carry_ref):
    tm = x_ref.shape[0]
    i = pl.program_id(0)

    @pl.when(i == 0)
    def _():
        carry_ref[...] = jnp.zeros_like(carry_ref)

    heads = []
    for hd in range(BRANCH_W // MX_HEAD_DIM):
        sl = slice(hd * MX_HEAD_DIM, (hd + 1) * MX_HEAD_DIM)
        sv = slice(BRANCH_W + hd * MX_HEAD_DIM, BRANCH_W + (hd + 1) * MX_HEAD_DIM)
        s = _dot_nt(cq_ref[:, sl], mkv_ref[:, sl])
        e = jnp.exp(s - jnp.max(s, axis=-1, keepdims=True))
        den = jnp.sum(e, axis=-1, keepdims=True)
        heads.append((_dot(e.astype(BF16), mkv_ref[:, sv]) / den).astype(BF16))
    oc = jnp.concatenate(heads, axis=1)

    ya = _dot(oa_ref[...], wa_ref[...])
    yb = _dot(ob_ref[...], wb_ref[...])
    yc = _dot(oc, wc_ref[...])
    merged = (gate_ref[:, 0:D_MODEL].astype(F32) * ya
              + gate_ref[:, D_MODEL:2 * D_MODEL].astype(F32) * yb
              + gate_ref[:, 2 * D_MODEL:3 * D_MODEL].astype(F32) * yc)
    xn = x_ref[...] + _dot(merged.astype(BF16), wo_ref[...])
    xo_ref[...] = xn
    h2 = _rms(xn, g2_ref[...])
    for c in range(CHUNKS):
        h2_ref[pl.ds(c, tm, stride=CHUNKS), :] = h2[:, c * LANES:(c + 1) * LANES]

    h_hi = h2.astype(BF16)
    h_lo = (h2 - h_hi.astype(F32)).astype(BF16)
    l2 = _dot_nt(wr_ref[...], h_hi)
    logits = (l2[:ROUTE_ROWS] + l2[ROUTE_ROWS:] + _dot_nt(wr_ref[:ROUTE_ROWS, :], h_lo)
              + br_ref[...])

    neg = -jnp.inf
    r8 = lax.broadcasted_iota(jnp.int32, (SUBLANES, tm), 0)
    r32 = lax.broadcasted_iota(jnp.int32, (N_EXPERTS, tm), 0)
    gl = jnp.where(r8 < N_GROUPS, logits[0:SUBLANES], neg)
    gmax = jnp.max(gl, axis=0, keepdims=True)
    gidx = jnp.min(jnp.where(gl == gmax, r8, SUBLANES), axis=0, keepdims=True)
    gp = 1.0 / jnp.sum(jnp.exp(gl - gmax), axis=0, keepdims=True)
    el = jnp.where((r32 // EXPERTS_PER_GROUP) == gidx, logits[SUBLANES:ROUTE_ROWS], neg)
    m1 = jnp.max(el, axis=0, keepdims=True)
    i1 = jnp.min(jnp.where(el == m1, r32, N_EXPERTS), axis=0, keepdims=True)
    el2 = jnp.where(r32 == i1, neg, el)
    m2 = jnp.max(el2, axis=0, keepdims=True)
    i2 = jnp.min(jnp.where(el2 == m2, r32, N_EXPERTS), axis=0, keepdims=True)
    d = jnp.exp(m2 - m1)
    w1 = gp / (1.0 + d)
    w2 = gp * d / (1.0 + d)

    hit1 = r32 == i1
    hit2 = r32 == i2
    oh = jnp.where(hit1 | hit2, 1.0, 0.0)
    before = _dot(oh.astype(BF16), tri_ref[...]) + carry_ref[...]
    rank1 = jnp.sum(jnp.where(hit1, before, 0.0), axis=0, keepdims=True)
    rank2 = jnp.sum(jnp.where(hit2, before, 0.0), axis=0, keepdims=True)
    carry_ref[...] = carry_ref[...] + jnp.sum(oh, axis=1, keepdims=True)
    zrow = jnp.zeros_like(w1)
    route_ref[...] = jnp.concatenate(
        [i1.astype(F32), i2.astype(F32), rank1, rank2, w1, w2, zrow, zrow], axis=0)
    cnt_ref[...] = jnp.broadcast_to(carry_ref[...], cnt_ref.shape)


def _post(x2d, oa, ob, cq, gate, mkv, wa, wb, wc, wo, g2, wr, br, tri):
    t = x2d.shape[0]
    tm = TM_POST
    nrb = SEQ // tm
    row = lambda i: (i, 0)
    const = lambda i: (0, 0)
    return pl.pallas_call(
        _post_kernel,
        out_shape=[jax.ShapeDtypeStruct((t, D_MODEL), F32),
                   jax.ShapeDtypeStruct((t * CHUNKS, LANES), F32),
                   jax.ShapeDtypeStruct((SUBLANES, t), F32),
                   jax.ShapeDtypeStruct((N_EXPERTS, LANES), F32)],
        grid=(t // tm,),
        in_specs=[pl.BlockSpec((tm, D_MODEL), row),
                  pl.BlockSpec((tm, BRANCH_W), row),
                  pl.BlockSpec((tm, BRANCH_W), row),
                  pl.BlockSpec((tm, BRANCH_W), row),
                  pl.BlockSpec((tm, 3 * D_MODEL), row),
                  pl.BlockSpec((MEM_LEN, 2 * BRANCH_W), lambda i: (i // nrb, 0)),
                  pl.BlockSpec(wa.shape, const), pl.BlockSpec(wb.shape, const),
                  pl.BlockSpec(wc.shape, const), pl.BlockSpec(wo.shape, const),
                  pl.BlockSpec((1, D_MODEL), const),
                  pl.BlockSpec(wr.shape, const), pl.BlockSpec(br.shape, const),
                  pl.BlockSpec(tri.shape, const)],
        out_specs=[pl.BlockSpec((tm, D_MODEL), row),
                   pl.BlockSpec((tm * CHUNKS, LANES), row),
                   pl.BlockSpec((SUBLANES, tm), lambda i: (0, i)),
                   pl.BlockSpec((N_EXPERTS, LANES), const)],
        scratch_shapes=[pltpu.VMEM((N_EXPERTS, 1), F32)],
        compiler_params=pltpu.CompilerParams(
            dimension_semantics=("arbitrary",), vmem_limit_bytes=VMEM_LIMIT),
        name="post",
    )(x2d, oa, ob, cq, gate, mkv, wa, wb, wc, wo, g2, wr, br, tri)


def _start_row_gather(idx_ref, n, src_ref, dst_ref, slot, sem):
    def body(i, carry):
        for u in range(ISSUE_UNROLL):
            r = i * ISSUE_UNROLL + u
            tok = idx_ref[0, 0, r]
            pltpu.make_async_copy(
                src_ref.at[pl.ds(pl.multiple_of(tok * CHUNKS, CHUNKS), CHUNKS)],
                dst_ref.at[slot, pl.ds(pl.multiple_of(r * CHUNKS, CHUNKS), CHUNKS)],
                sem.at[slot]).start(priority=u % 2)
        return carry
    lax.fori_loop(0, n // ISSUE_UNROLL, body, 0)


def _wait_row_gather(n, src_ref, dst_ref, slot, sem):
    pltpu.make_async_copy(src_ref.at[pl.ds(0, n * CHUNKS)], dst_ref.at[slot], sem.at[slot]).wait()


def _rows_from_tiles(buf_ref, slot, first, n):
    return jnp.concatenate(
        [buf_ref[slot, pl.ds(first * CHUNKS + c, n, stride=CHUNKS), :] for c in range(CHUNKS)],
        axis=1)


def _dispatch_kernel(idx_ref, h2_ref, xin_ref, xbuf_ref, sem):
    del xin_ref
    n = idx_ref.shape[2]

    def body(i, carry):
        for u in range(ISSUE_UNROLL):
            r = i * ISSUE_UNROLL + u
            slot_row = idx_ref[0, 0, r]
            pltpu.make_async_copy(
                h2_ref.at[pl.ds(pl.multiple_of((r // TOP_K) * CHUNKS, CHUNKS), CHUNKS)],
                xbuf_ref.at[pl.ds(pl.multiple_of(slot_row * CHUNKS, CHUNKS), CHUNKS)],
                sem.at[0]).start(priority=u % 2)
        return carry
    lax.fori_loop(0, n // ISSUE_UNROLL, body, 0)
    for _ in range(TOP_K):
        pltpu.make_async_copy(h2_ref, xbuf_ref.at[pl.ds(0, h2_ref.shape[0])], sem.at[0]).wait()


def _dispatch(slots, h2, xbuf_init):
    nsteps, _, n = slots.shape
    td = n // TOP_K
    return pl.pallas_call(
        _dispatch_kernel,
        out_shape=jax.ShapeDtypeStruct(xbuf_init.shape, F32),
        grid=(nsteps,),
        in_specs=[pl.BlockSpec((1, 1, n), lambda i: (i, 0, 0), memory_space=pltpu.SMEM),
                  pl.BlockSpec((td * CHUNKS, LANES), lambda i: (i, 0)),
                  pl.BlockSpec(memory_space=pl.ANY)],
        out_specs=pl.BlockSpec(memory_space=pl.ANY),
        input_output_aliases={2: 0},
        scratch_shapes=[pltpu.SemaphoreType.DMA((1,))],
        compiler_params=pltpu.CompilerParams(
            dimension_semantics=("arbitrary",), vmem_limit_bytes=VMEM_LIMIT),
        name="dispatch",
    )(slots, h2, xbuf_init)


def _moe_kernel(bexp_ref, bval_ref, nb_ref, x_ref, wg_ref, wu_ref, wd_ref, y_ref,
                wgb, wub, wdb):
    b = pl.program_id(0)

    @pl.when((b == 0) | (bexp_ref[b] != bexp_ref[jnp.maximum(b - 1, 0)]))
    def _():
        wgb[...] = wg_ref[0, 0].astype(BF16)
        wub[...] = wu_ref[0, 0].astype(BF16)
        wdb[...] = wd_ref[0, 0].astype(BF16)

    @pl.when(b >= nb_ref[0])
    def _():
        y_ref[...] = jnp.zeros_like(y_ref)

    @pl.when(b < nb_ref[0])
    def _():
        xf = jnp.concatenate(
            [x_ref[pl.ds(c, BM, stride=CHUNKS), :] for c in range(CHUNKS)], axis=1)
        rowid = lax.broadcasted_iota(jnp.int32, (BM, 1), 0)
        xb = jnp.where(rowid < bval_ref[b], xf, 0.0).astype(BF16)
        hg = _dot(xb, wgb[...])
        hu = _dot(xb, wub[...])
        hid = (hg / (1.0 + jnp.exp(-hg)) * hu).astype(BF16)
        y = _dot(hid, wdb[...])
        for c in range(CHUNKS):
            y_ref[pl.ds(c, BM, stride=CHUNKS), :] = y[:, c * LANES:(c + 1) * LANES]


def _moe(bexp, bval, nb_used, xbuf, wg, wu, wd, layer):
    nblk = bexp.shape[0]
    wmap = lambda b, bexp, bval, nb: (layer, bexp[b], 0, 0)
    xmap = lambda b, bexp, bval, nb: (jnp.minimum(b, nb[0] - 1), 0)
    return pl.pallas_call(
        _moe_kernel,
        out_shape=jax.ShapeDtypeStruct(xbuf.shape, F32),
        grid_spec=pltpu.PrefetchScalarGridSpec(
            num_scalar_prefetch=3,
            grid=(nblk,),
            in_specs=[
                pl.BlockSpec((BM * CHUNKS, LANES), xmap),
                pl.BlockSpec((1, 1, D_MODEL, D_EXPERT), wmap),
                pl.BlockSpec((1, 1, D_MODEL, D_EXPERT), wmap),
                pl.BlockSpec((1, 1, D_EXPERT, D_MODEL), wmap),
            ],
            out_specs=pl.BlockSpec((BM * CHUNKS, LANES), lambda b, bexp, bval, nb: (b, 0)),
            scratch_shapes=[pltpu.VMEM((D_MODEL, D_EXPERT), BF16),
                            pltpu.VMEM((D_MODEL, D_EXPERT), BF16),
                            pltpu.VMEM((D_EXPERT, D_MODEL), BF16)]),
        compiler_params=pltpu.CompilerParams(
            dimension_semantics=("arbitrary",), vmem_limit_bytes=VMEM_LIMIT),
        name="experts",
    )(bexp, bval, nb_used, xbuf, wg, wu, wd)


def _comb_kernel(cur_ref, nxt_ref, x_ref, wt_ref, fg_ref, y_ref, o_ref, ybuf, sem, *, final):
    tm = x_ref.shape[0]
    i = pl.program_id(0)
    n = pl.num_programs(0)
    slot = i % 2

    @pl.when(i == 0)
    def _():
        _start_row_gather(cur_ref, 2 * tm, y_ref, ybuf, 0, sem)

    @pl.when(i + 1 < n)
    def _():
        _start_row_gather(nxt_ref, 2 * tm, y_ref, ybuf, 1 - slot, sem)

    def compute(s):
        _wait_row_gather(2 * tm, y_ref, ybuf, s, sem)
        y0 = _rows_from_tiles(ybuf, s, 0, tm)
        y1 = _rows_from_tiles(ybuf, s, tm, tm)
        wt = wt_ref[...]
        xo = x_ref[...] + wt[:, 4:5] * y0 + wt[:, 5:6] * y1
        if final:
            xo = _rms(xo, fg_ref[...])
        o_ref[...] = xo

    for s in range(2):
        @pl.when(slot == s)
        def _(s=s):
            compute(s)


def _combine(dest, x2d, wt, fg, ybuf, *, final):
    t = x2d.shape[0]
    tm = TM_COMB
    nsteps = t // tm
    row = lambda i: (i, 0)
    return pl.pallas_call(
        functools.partial(_comb_kernel, final=final),
        out_shape=jax.ShapeDtypeStruct((t, D_MODEL), F32),
        grid=(nsteps,),
        in_specs=[
            pl.BlockSpec((1, 1, 2 * tm), lambda i: (i, 0, 0), memory_space=pltpu.SMEM),
            pl.BlockSpec((1, 1, 2 * tm), lambda i: (jnp.minimum(i + 1, nsteps - 1), 0, 0),
                         memory_space=pltpu.SMEM),
            pl.BlockSpec((tm, D_MODEL), row),
            pl.BlockSpec((tm, SUBLANES), row),
            pl.BlockSpec((1, D_MODEL), lambda i: (0, 0)),
            pl.BlockSpec(memory_space=pl.ANY),
        ],
        out_specs=pl.BlockSpec((tm, D_MODEL), row),
        scratch_shapes=[pltpu.VMEM((2, 2 * tm * CHUNKS, LANES), F32),
                        pltpu.SemaphoreType.DMA((2,))],
        compiler_params=pltpu.CompilerParams(
            dimension_semantics=("arbitrary",), vmem_limit_bytes=VMEM_LIMIT),
        name="combine",
    )(dest, dest, x2d, wt, fg, ybuf)


def _rope_tables():
    pos = jnp.arange(SEQ, dtype=F32)
    inv = ROPE_THETA ** (-jnp.arange(0, HEAD_DIM, 2, dtype=F32) / HEAD_DIM)
    ang1 = pos[:, None] * inv[None, :]
    rows = SEQ // GRID_W
    r = jnp.broadcast_to(jnp.arange(rows, dtype=F32)[:, None], (rows, GRID_W)).reshape(-1)
    c = jnp.broadcast_to(jnp.arange(GRID_W, dtype=F32)[None, :], (rows, GRID_W)).reshape(-1)
    axis_dim = HEAD_DIM // 2
    inv2 = ROPE_THETA ** (-jnp.arange(0, axis_dim, 2, dtype=F32) / axis_dim)
    ang2 = jnp.concatenate([r[:, None] * inv2, c[:, None] * inv2], axis=-1)

    def tables(ang):
        cs, sn = jnp.cos(ang), jnp.sin(ang)
        reps = LANES // HEAD_DIM
        return (jnp.tile(jnp.concatenate([cs, cs], axis=1), (1, reps)),
                jnp.tile(jnp.concatenate([-sn, sn], axis=1), (1, reps)))

    return tables(ang1) + tables(ang2)


def kernel(x, mem, mem_norm_g, w_mem_kv, norm1_g, w_in, lam_q1, lam_k1, lam_q2, lam_k2,
           subln_g, q_norm_g, k_norm_g, w_up_a, w_up_b, w_up_c, w_out, norm2_g,
           w_router_group, b_router_group, w_router_expert, b_router_expert,
           w_exp_gate, w_exp_up, w_exp_down, final_norm_g):
    batch, seq, d = x.shape
    depth = w_in.shape[0]
    assert (seq, d, mem.shape[1]) == (SEQ, D_MODEL, MEM_LEN)
    t = batch * seq
    ca, sa, cb, sb = _rope_tables()
    gidx = jnp.arange(512) // HEAD_DIM
    bd = (gidx[:, None] == gidx[None, :]).astype(BF16)
    ti = jnp.arange(TM_POST)
    tri = (ti[:, None] < ti[None, :]).astype(BF16)
    nblk = (t * TOP_K) // BM + N_EXPERTS

    mkv = _memkv(mem.reshape(batch * MEM_LEN, d), mem_norm_g.reshape(1, d), w_mem_kv.astype(BF16))
    x2d = x.reshape(t, d)
    ybuf = None
    for l in range(depth):
        wl = w_in[l]
        bk0, bk1 = wl[:, 2048:2112], wl[:, 2112:2176]
        bv0, bv1 = wl[:, 2176:2240], wl[:, 2240:2304]
        w = jnp.concatenate([wl[:, :2048], bk0, bk0, bk1, bk1, bv0, bv0, bv1, bv1, wl[:, 2304:]],
                            axis=1).astype(BF16)
        aq, ak, av, bq, bk, bv, cq, gate = _in_proj(
            x2d, norm1_g[l].reshape(1, d), w, ca, sa, cb, sb,
            jnp.tile(q_norm_g[l], 8).reshape(1, 512), jnp.tile(k_norm_g[l], 4).reshape(1, 256), bd)

        lam_init = 0.8 - 0.6 * math.exp(-0.3 * l)
        lamp = jnp.stack([lam_q1[l], lam_k1[l], lam_q2[l], lam_k2[l]]).astype(F32)
        oa = _attention(aq, ak, av, batch, diff=True, lamp=lamp,
                        gs=subln_g[l].reshape(LANES, 1), lam_init=lam_init)
        ob = _attention(bq, bk, bv, batch, diff=False)

        wr = jnp.zeros((ROUTE_ROWS, d), F32)
        wr = wr.at[0:N_GROUPS].set(w_router_group[l].T).at[SUBLANES:].set(w_router_expert[l].T)
        wr_hi = wr.astype(BF16)
        wr_lo = (wr - wr_hi.astype(F32)).astype(BF16)
        br = jnp.zeros((ROUTE_ROWS, 1), F32)
        br = br.at[0:N_GROUPS, 0].set(b_router_group[l]).at[SUBLANES:, 0].set(b_router_expert[l])
        x2d, h2, route, cnt = _post(
            x2d, oa, ob, cq, gate, mkv, w_up_a[l].astype(BF16), w_up_b[l].astype(BF16),
            w_up_c[l].astype(BF16), w_out[l].astype(BF16), norm2_g[l].reshape(1, d),
            jnp.concatenate([wr_hi, wr_lo], axis=0), br, tri)

        counts = cnt[:, 0].astype(jnp.int32)
        nblk_e = (counts + BM - 1) // BM
        cum = jnp.cumsum(nblk_e)
        first = cum - nblk_e
        nb_used = cum[-1].reshape(1)
        bidx = jnp.minimum(jnp.arange(nblk, dtype=jnp.int32), nb_used - 1)
        bexp = jnp.sum((cum[None, :] <= bidx[:, None]).astype(jnp.int32), axis=1)
        bval = jnp.clip(counts[bexp] - (bidx - first[bexp]) * BM, 0, BM)
        experts = route[0:2].astype(jnp.int32)
        hit = experts[:, :, None] == jnp.arange(N_EXPERTS, dtype=jnp.int32)
        slots = (jnp.sum(jnp.where(hit, first * BM, 0), axis=-1)
                 + route[2:4].astype(jnp.int32))

        if ybuf is None:
            ybuf = jnp.zeros((nblk * BM * CHUNKS, LANES), F32)
        xbuf = _dispatch(slots.T.reshape(t // TD, 1, TOP_K * TD), h2, ybuf)
        ybuf = _moe(bexp, bval, nb_used, xbuf, w_exp_gate, w_exp_up, w_exp_down, l)

        dest_blk = slots.reshape(2, t // TM_COMB, TM_COMB).transpose(1, 0, 2).reshape(
            t // TM_COMB, 1, 2 * TM_COMB)
        x2d = _combine(dest_blk, x2d, route.T, final_norm_g.reshape(1, d), ybuf,
                       final=(l == depth - 1))
    return x2d.reshape(batch, seq, d)
```

```python
import functools
import math

import jax
import jax.numpy as jnp
from jax import lax
from jax.experimental import pallas as pl
from jax.experimental.pallas import tpu as pltpu

F32 = jnp.float32
BF16 = jnp.bfloat16

D_MODEL = 1024
SEQ = 2048
MEM_LEN = 256
HEAD_DIM = 64
MX_HEAD_DIM = 128
BRANCH_W = 512
GRID_W = 64
ROPE_THETA = 10000.0
NORM_EPS = 1e-6
N_GROUPS = 4
EXPERTS_PER_GROUP = 8
N_EXPERTS = N_GROUPS * EXPERTS_PER_GROUP
TOP_K = 2
D_EXPERT = 512

LANES = 128
SUBLANES = 8
CHUNKS = D_MODEL // LANES

TM_IN = 256
TQ = 1024
TK = 256
ONES_ROWS = 16
LOG2E = math.log2(math.e)
TM_POST = 256
BM = 256
TM_COMB = 256
TD = 1024
ISSUE_UNROLL = 8
VMEM_LIMIT = 56 * 1024 * 1024

C_AQ, C_AK, C_AV, C_BQ, C_BK, C_BV, C_CQ, C_G, C_END = (
    0, 512, 1024, 1536, 2048, 2304, 2560, 3072, 6144)
ROUTE_ROWS = 40


def _rms(xf, g):
    ms = jnp.mean(xf * xf, axis=-1, keepdims=True)
    return xf * lax.rsqrt(ms + NORM_EPS) * g


def _dot(a, b):
    return jnp.dot(a, b, preferred_element_type=F32)


def _dot_nt(a, b):
    return lax.dot_general(a, b, (((1,), (1,)), ((), ())), preferred_element_type=F32)


def _memkv_kernel(m_ref, g_ref, w_ref, o_ref):
    h = _rms(m_ref[...], g_ref[...]).astype(BF16)
    o_ref[...] = _dot(h, w_ref[...]).astype(BF16)


def _memkv(mem2d, g, w):
    n = mem2d.shape[0]
    tm = 512
    return pl.pallas_call(
        _memkv_kernel,
        out_shape=jax.ShapeDtypeStruct((n, w.shape[1]), BF16),
        grid=(n // tm,),
        in_specs=[pl.BlockSpec((tm, D_MODEL), lambda i: (i, 0)),
                  pl.BlockSpec((1, D_MODEL), lambda i: (0, 0)),
                  pl.BlockSpec(w.shape, lambda i: (0, 0))],
        out_specs=pl.BlockSpec((tm, w.shape[1]), lambda i: (i, 0)),
        compiler_params=pltpu.CompilerParams(
            dimension_semantics=("arbitrary",), vmem_limit_bytes=VMEM_LIMIT),
        name="memkv",
    )(mem2d, g, w)


def _in_kernel(x_ref, g1_ref, w_ref, ca_ref, sa_ref, cb_ref, sb_ref, qg_ref, kg_ref, bd_ref,
               aq_ref, ak_ref, av_ref, bq_ref, bk_ref, bv_ref, cq_ref, gate_ref):
    tm = x_ref.shape[0]
    h = _rms(x_ref[...], g1_ref[...]).astype(BF16)
    lane = lax.broadcasted_iota(jnp.int32, (tm, LANES), 1)
    first_half = (lane & (HEAD_DIM // 2)) == 0

    def seg(lo, hi):
        return _dot(h, w_ref[:, lo:hi])

    def rope(p, c, s):
        sw = jnp.where(first_half, pltpu.roll(p, LANES - HEAD_DIM // 2, 1),
                       pltpu.roll(p, HEAD_DIM // 2, 1))
        return p * c + sw * s

    def group_norm(p, gain):
        n = p.shape[1]
        ss = _dot((p * p).astype(BF16), bd_ref[:n, :n])
        return p * lax.rsqrt(ss * (1.0 / HEAD_DIM) + NORM_EPS) * gain

    def rope_store(p, c_ref, s_ref, o_ref, scale):
        c = c_ref[...]
        s = s_ref[...]
        for j in range(p.shape[1] // LANES):
            sl = slice(j * LANES, (j + 1) * LANES)
            o_ref[:, sl] = (rope(p[:, sl], c, s) * scale).astype(BF16)

    q_scale = HEAD_DIM ** -0.5 * LOG2E
    rope_store(seg(C_AQ, C_AK), ca_ref, sa_ref, aq_ref, q_scale)
    rope_store(seg(C_AK, C_AV), ca_ref, sa_ref, ak_ref, 1.0)
    av_ref[...] = seg(C_AV, C_BQ).T.astype(BF16)
    rope_store(group_norm(seg(C_BQ, C_BK), qg_ref[...]), cb_ref, sb_ref, bq_ref, q_scale)
    rope_store(group_norm(seg(C_BK, C_BV), kg_ref[...]), cb_ref, sb_ref, bk_ref, 1.0)
    bv_ref[...] = seg(C_BV, C_CQ).T.astype(BF16)
    cq_ref[...] = (seg(C_CQ, C_G) * (MX_HEAD_DIM ** -0.5)).astype(BF16)
    for j in range((C_END - C_G) // 512):
        lo = C_G + j * 512
        z = seg(lo, lo + 512)
        gate_ref[:, j * 512:(j + 1) * 512] = (1.0 / (1.0 + jnp.exp(-z))).astype(BF16)


def _in_proj(x2d, g1, w, ca, sa, cb, sb, qg, kg, bd):
    t = x2d.shape[0]
    tm = TM_IN
    nrb = SEQ // tm
    row = lambda i: (i, 0)
    const = lambda i: (0, 0)
    tab = lambda i: (i % nrb, 0)
    outs = ((512, False), (512, False), (512, True), (512, False), (256, False), (256, True),
            (512, False), (3072, False))
    col = lambda i: (0, i)
    return pl.pallas_call(
        _in_kernel,
        out_shape=[jax.ShapeDtypeStruct((n, t) if tr else (t, n), BF16) for n, tr in outs],
        grid=(t // tm,),
        in_specs=[pl.BlockSpec((tm, D_MODEL), row),
                  pl.BlockSpec((1, D_MODEL), const),
                  pl.BlockSpec(w.shape, const),
                  pl.BlockSpec((tm, LANES), tab), pl.BlockSpec((tm, LANES), tab),
                  pl.BlockSpec((tm, LANES), tab), pl.BlockSpec((tm, LANES), tab),
                  pl.BlockSpec((1, 512), const), pl.BlockSpec((1, 256), const),
                  pl.BlockSpec((512, 512), const)],
        out_specs=[pl.BlockSpec((n, tm), col) if tr else pl.BlockSpec((tm, n), row)
                   for n, tr in outs],
        compiler_params=pltpu.CompilerParams(
            dimension_semantics=("arbitrary",), vmem_limit_bytes=VMEM_LIMIT),
        name="in_proj",
    )(x2d, g1, w, ca, sa, cb, sb, qg, kg, bd)


def _attn_kernel(*refs, diff, post_scale, lam_init):
    if diff:
        lamp_ref, gs_ref, q_ref, k_ref, vt_ref, o_ref = refs
    else:
        q_ref, k_ref, vt_ref, o_ref = refs
    tq = q_ref.shape[0]
    q = q_ref[...]
    lane = lax.broadcasted_iota(jnp.int32, (tq, LANES), 1)
    lo = lane < HEAD_DIM
    zero = jnp.zeros_like(q)
    qs = jnp.concatenate([jnp.where(lo, q, zero), jnp.where(lo, zero, q)], axis=0)
    cols = 2 * tq
    ones = jnp.ones((ONES_ROWS, TK), BF16)
    m = jnp.full((1, cols), -jnp.inf, F32)
    acc = jnp.zeros((LANES + ONES_ROWS, cols), F32)
    nchunks = SEQ // TK

    def scores(j):
        return _dot_nt(k_ref[j * TK:(j + 1) * TK, :], qs)

    st_next = scores(0)
    for j in range(nchunks):
        st = st_next
        if j + 1 < nchunks:
            st_next = scores(j + 1)
        vtj = jnp.concatenate([vt_ref[:, j * TK:(j + 1) * TK], ones], axis=0)
        m_new = jnp.maximum(m, jnp.max(st, axis=0, keepdims=True))
        alpha = jnp.exp2(m - m_new)
        e = jnp.exp2(st - m_new).astype(BF16)
        acc = alpha * acc + _dot(vtj, e)
        m = m_new
    o = acc[:LANES] / acc[LANES:LANES + 1]
    if diff:
        lp = lamp_ref[...]
        lam = (jnp.exp(jnp.sum(lp[0:1] * lp[1:2], axis=-1, keepdims=True))
               - jnp.exp(jnp.sum(lp[2:3] * lp[3:4], axis=-1, keepdims=True)) + lam_init)
        d = o[:, :tq] - lam * o[:, tq:]
        ms = jnp.mean(d * d, axis=0, keepdims=True)
        out_t = d * lax.rsqrt(ms + NORM_EPS) * gs_ref[...] * post_scale
    else:
        row = lax.broadcasted_iota(jnp.int32, (LANES, tq), 0)
        out_t = jnp.where(row < HEAD_DIM, o[:, :tq], o[:, tq:])
    o_ref[...] = out_t.T.astype(BF16)


def _attention(q, k, vt, batch, *, diff, lamp=None, gs=None, lam_init=0.0):
    t = q.shape[0]
    nq = SEQ // TQ
    nblk = q.shape[1] // LANES
    kv_per = nblk // (k.shape[1] // LANES)
    qmap = lambda b, h, i: (b * nq + i, h)
    in_specs = [pl.BlockSpec((TQ, LANES), qmap),
                pl.BlockSpec((SEQ, LANES), lambda b, h, i: (b, h // kv_per)),
                pl.BlockSpec((LANES, SEQ), lambda b, h, i: (h // kv_per, b))]
    args = [q, k, vt]
    if diff:
        const = lambda b, h, i: (0, 0)
        in_specs = [pl.BlockSpec((4, HEAD_DIM), const), pl.BlockSpec((LANES, 1), const)] + in_specs
        args = [lamp, gs] + args
    return pl.pallas_call(
        functools.partial(_attn_kernel, diff=diff, post_scale=1.0 - lam_init, lam_init=lam_init),
        out_shape=jax.ShapeDtypeStruct((t, q.shape[1]), BF16),
        grid=(batch, nblk, nq),
        in_specs=in_specs,
        out_specs=pl.BlockSpec((TQ, LANES), qmap),
        compiler_params=pltpu.CompilerParams(
            dimension_semantics=("arbitrary", "arbitrary", "arbitrary"),
            vmem_limit_bytes=VMEM_LIMIT),
        name="attn_diff" if diff else "attn_gqa",
    )(*args)


def _post_kernel(x_ref, oa_ref, ob_ref, cq_ref, gate_ref, mkv_ref, wa_ref, wb_ref, wc_ref,
                 wo_ref, g2_ref, wr_ref, br_ref, tri_ref,
                 xo_ref, h2_ref, route_ref, cnt_ref, carry_ref):
    tm = x_ref.shape[0]
    i = pl.program_id(0)

    @pl.when(i == 0)
    def _():
        carry_ref[...] = jnp.zeros_like(carry_ref)

    heads = []
    for hd in range(BRANCH_W // MX_HEAD_DIM):
        sl = slice(hd * MX_HEAD_DIM, (hd + 1) * MX_HEAD_DIM)
        sv = slice(BRANCH_W + hd * MX_HEAD_DIM, BRANCH_W + (hd + 1) * MX_HEAD_DIM)
        s = _dot_nt(cq_ref[:, sl], mkv_ref[:, sl])
        e = jnp.exp(s - jnp.max(s, axis=-1, keepdims=True))
        den = jnp.sum(e, axis=-1, keepdims=True)
        heads.append((_dot(e.astype(BF16), mkv_ref[:, sv]) / den).astype(BF16))
    oc = jnp.concatenate(heads, axis=1)

    ya = _dot(oa_ref[...], wa_ref[...])
    yb = _dot(ob_ref[...], wb_ref[...])
    yc = _dot(oc, wc_ref[...])
    merged = (gate_ref[:, 0:D_MODEL].astype(F32) * ya
              + gate_ref[:, D_MODEL:2 * D_MODEL].astype(F32) * yb
              + gate_ref[:, 2 * D_MODEL:3 * D_MODEL].astype(F32) * yc)
    xn = x_ref[...] + _dot(merged.astype(BF16), wo_ref[...])
    xo_ref[...] = xn
    h2 = _rms(xn, g2_ref[...])
    for c in range(CHUNKS):
        h2_ref[pl.ds(c, tm, stride=CHUNKS), :] = h2[:, c * LANES:(c + 1) * LANES]

    h_hi = h2.astype(BF16)
    h_lo = (h2 - h_hi.astype(F32)).astype(BF16)
    l2 = _dot_nt(wr_ref[...], h_hi)
    logits = (l2[:ROUTE_ROWS] + l2[ROUTE_ROWS:] + _dot_nt(wr_ref[:ROUTE_ROWS, :], h_lo)
              + br_ref[...])

    neg = -jnp.inf
    r8 = lax.broadcasted_iota(jnp.int32, (SUBLANES, tm), 0)
    r32 = lax.broadcasted_iota(jnp.int32, (N_EXPERTS, tm), 0)
    gl = jnp.where(r8 < N_GROUPS, logits[0:SUBLANES], neg)
    gmax = jnp.max(gl, axis=0, keepdims=True)
    gidx = jnp.min(jnp.where(gl == gmax, r8, SUBLANES), axis=0, keepdims=True)
    gp = 1.0 / jnp.sum(jnp.exp(gl - gmax), axis=0, keepdims=True)
    el = jnp.where((r32 // EXPERTS_PER_GROUP) == gidx, logits[SUBLANES:ROUTE_ROWS], neg)
    m1 = jnp.max(el, axis=0, keepdims=True)
    i1 = jnp.min(jnp.where(el == m1, r32, N_EXPERTS), axis=0, keepdims=True)
    el2 = jnp.where(r32 == i1, neg, el)
    m2 = jnp.max(el2, axis=0, keepdims=True)
    i2 = jnp.min(jnp.where(el2 == m2, r32, N_EXPERTS), axis=0, keepdims=True)
    d = jnp.exp(m2 - m1)
    w1 = gp / (1.0 + d)
    w2 = gp * d / (1.0 + d)

    hit1 = r32 == i1
    hit2 = r32 == i2
    oh = jnp.where(hit1 | hit2, 1.0, 0.0)
    before = _dot(oh.astype(BF16), tri_ref[...]) + carry_ref[...]
    rank1 = jnp.sum(jnp.where(hit1, before, 0.0), axis=0, keepdims=True)
    rank2 = jnp.sum(jnp.where(hit2, before, 0.0), axis=0, keepdims=True)
    carry_ref[...] = carry_ref[...] + jnp.sum(oh, axis=1, keepdims=True)
    zrow = jnp.zeros_like(w1)
    route_ref[...] = jnp.concatenate(
        [i1.astype(F32), i2.astype(F32), rank1, rank2, w1, w2, zrow, zrow], axis=0)
    cnt_ref[...] = jnp.broadcast_to(carry_ref[...], cnt_ref.shape)


def _post(x2d, oa, ob, cq, gate, mkv, wa, wb, wc, wo, g2, wr, br, tri):
    t = x2d.shape[0]
    tm = TM_POST
    nrb = SEQ // tm
    row = lambda i: (i, 0)
    const = lambda i: (0, 0)
    return pl.pallas_call(
        _post_kernel,
        out_shape=[jax.ShapeDtypeStruct((t, D_MODEL), F32),
                   jax.ShapeDtypeStruct((t * CHUNKS, LANES), F32),
                   jax.ShapeDtypeStruct((SUBLANES, t), F32),
                   jax.ShapeDtypeStruct((N_EXPERTS, LANES), F32)],
        grid=(t // tm,),
        in_specs=[pl.BlockSpec((tm, D_MODEL), row),
                  pl.BlockSpec((tm, BRANCH_W), row),
                  pl.BlockSpec((tm, BRANCH_W), row),
                  pl.BlockSpec((tm, BRANCH_W), row),
                  pl.BlockSpec((tm, 3 * D_MODEL), row),
                  pl.BlockSpec((MEM_LEN, 2 * BRANCH_W), lambda i: (i // nrb, 0)),
                  pl.BlockSpec(wa.shape, const), pl.BlockSpec(wb.shape, const),
                  pl.BlockSpec(wc.shape, const), pl.BlockSpec(wo.shape, const),
                  pl.BlockSpec((1, D_MODEL), const),
                  pl.BlockSpec(wr.shape, const), pl.BlockSpec(br.shape, const),
                  pl.BlockSpec(tri.shape, const)],
        out_specs=[pl.BlockSpec((tm, D_MODEL), row),
                   pl.BlockSpec((tm * CHUNKS, LANES), row),
                   pl.BlockSpec((SUBLANES, tm), lambda i: (0, i)),
                   pl.BlockSpec((N_EXPERTS, LANES), const)],
        scratch_shapes=[pltpu.VMEM((N_EXPERTS, 1), F32)],
        compiler_params=pltpu.CompilerParams(
            dimension_semantics=("arbitrary",), vmem_limit_bytes=VMEM_LIMIT),
        name="post",
    )(x2d, oa, ob, cq, gate, mkv, wa, wb, wc, wo, g2, wr, br, tri)


def _start_row_gather(idx_ref, n, src_ref, dst_ref, slot, sem):
    def body(i, carry):
        for u in range(ISSUE_UNROLL):
            r = i * ISSUE_UNROLL + u
            tok = idx_ref[0, 0, r]
            pltpu.make_async_copy(
                src_ref.at[pl.ds(pl.multiple_of(tok * CHUNKS, CHUNKS), CHUNKS)],
                dst_ref.at[slot, pl.ds(pl.multiple_of(r * CHUNKS, CHUNKS), CHUNKS)],
                sem.at[slot]).start(priority=u % 2)
        return carry
    lax.fori_loop(0, n // ISSUE_UNROLL, body, 0)


def _wait_row_gather(n, src_ref, dst_ref, slot, sem):
    pltpu.make_async_copy(src_ref.at[pl.ds(0, n * CHUNKS)], dst_ref.at[slot], sem.at[slot]).wait()


def _rows_from_tiles(buf_ref, slot, first, n):
    return jnp.concatenate(
        [buf_ref[slot, pl.ds(first * CHUNKS + c, n, stride=CHUNKS), :] for c in range(CHUNKS)],
        axis=1)


def _dispatch_kernel(idx_ref, h2_ref, xin_ref, xbuf_ref, sem):
    del xin_ref
    n = idx_ref.shape[2]
    i = pl.program_id(0)
    last = pl.num_programs(0) - 1
    tok0 = i * (n // TOP_K)

    def issue(slot):
        def body(j, carry):
            for u in range(ISSUE_UNROLL):
                r = j * ISSUE_UNROLL + u
                slot_row = idx_ref[0, 0, r]
                pltpu.make_async_copy(
                    h2_ref.at[pl.ds(pl.multiple_of((tok0 + r // TOP_K) * CHUNKS, CHUNKS), CHUNKS)],
                    xbuf_ref.at[pl.ds(pl.multiple_of(slot_row * CHUNKS, CHUNKS), CHUNKS)],
                    sem.at[slot]).start()
            return carry
        lax.fori_loop(0, n // ISSUE_UNROLL, body, 0)

    def wait(slot):
        pltpu.make_async_copy(h2_ref.at[pl.ds(0, n * CHUNKS)], xbuf_ref.at[pl.ds(0, n * CHUNKS)],
                              sem.at[slot]).wait()

    for s in range(2):
        @pl.when(i % 2 == s)
        def _(s=s):
            issue(s)

            @pl.when(i > 0)
            def _():
                wait(1 - s)

            @pl.when(i == last)
            def _():
                wait(s)


def _dispatch(slots, h2, xbuf_init):
    nsteps, _, n = slots.shape
    return pl.pallas_call(
        _dispatch_kernel,
        out_shape=jax.ShapeDtypeStruct(xbuf_init.shape, F32),
        grid=(nsteps,),
        in_specs=[pl.BlockSpec((1, 1, n), lambda i: (i, 0, 0), memory_space=pltpu.SMEM),
                  pl.BlockSpec(memory_space=pl.ANY),
                  pl.BlockSpec(memory_space=pl.ANY)],
        out_specs=pl.BlockSpec(memory_space=pl.ANY),
        input_output_aliases={2: 0},
        scratch_shapes=[pltpu.SemaphoreType.DMA((2,))],
        compiler_params=pltpu.CompilerParams(
            dimension_semantics=("arbitrary",), vmem_limit_bytes=VMEM_LIMIT),
        name="dispatch",
    )(slots, h2, xbuf_init)


def _moe_kernel(bexp_ref, bval_ref, nb_ref, x_ref, wg_ref, wu_ref, wd_ref, y_ref,
                wgb, wub, wdb):
    b = pl.program_id(0)

    @pl.when((b == 0) | (bexp_ref[b] != bexp_ref[jnp.maximum(b - 1, 0)]))
    def _():
        wgb[...] = wg_ref[0, 0].astype(BF16)
        wub[...] = wu_ref[0, 0].astype(BF16)
        wdb[...] = wd_ref[0, 0].astype(BF16)

    @pl.when(b >= nb_ref[0])
    def _():
        y_ref[...] = jnp.zeros_like(y_ref)

    @pl.when(b < nb_ref[0])
    def _():
        xf = jnp.concatenate(
            [x_ref[pl.ds(c, BM, stride=CHUNKS), :] for c in range(CHUNKS)], axis=1)
        rowid = lax.broadcasted_iota(jnp.int32, (BM, 1), 0)
        xb = jnp.where(rowid < bval_ref[b], xf, 0.0).astype(BF16)
        hg = _dot(xb, wgb[...])
        hu = _dot(xb, wub[...])
        hid = (hg / (1.0 + jnp.exp(-hg)) * hu).astype(BF16)
        y = _dot(hid, wdb[...])
        for c in range(CHUNKS):
            y_ref[pl.ds(c, BM, stride=CHUNKS), :] = y[:, c * LANES:(c + 1) * LANES]


def _moe(bexp, bval, nb_used, xbuf, wg, wu, wd, layer):
    nblk = bexp.shape[0]
    wmap = lambda b, bexp, bval, nb: (layer, bexp[b], 0, 0)
    xmap = lambda b, bexp, bval, nb: (jnp.minimum(b, nb[0] - 1), 0)
    return pl.pallas_call(
        _moe_kernel,
        out_shape=jax.ShapeDtypeStruct(xbuf.shape, F32),
        grid_spec=pltpu.PrefetchScalarGridSpec(
            num_scalar_prefetch=3,
            grid=(nblk,),
            in_specs=[
                pl.BlockSpec((BM * CHUNKS, LANES), xmap),
                pl.BlockSpec((1, 1, D_MODEL, D_EXPERT), wmap),
                pl.BlockSpec((1, 1, D_MODEL, D_EXPERT), wmap),
                pl.BlockSpec((1, 1, D_EXPERT, D_MODEL), wmap),
            ],
            out_specs=pl.BlockSpec((BM * CHUNKS, LANES), lambda b, bexp, bval, nb: (b, 0)),
            scratch_shapes=[pltpu.VMEM((D_MODEL, D_EXPERT), BF16),
                            pltpu.VMEM((D_MODEL, D_EXPERT), BF16),
                            pltpu.VMEM((D_EXPERT, D_MODEL), BF16)]),
        compiler_params=pltpu.CompilerParams(
            dimension_semantics=("arbitrary",), vmem_limit_bytes=VMEM_LIMIT),
        name="experts",
    )(bexp, bval, nb_used, xbuf, wg, wu, wd)


def _comb_kernel(cur_ref, nxt_ref, x_ref, wt_ref, fg_ref, y_ref, o_ref, ybuf, sem, *, final):
    tm = x_ref.shape[0]
    i = pl.program_id(0)
    n = pl.num_programs(0)
    slot = i % 2

    @pl.when(i == 0)
    def _():
        _start_row_gather(cur_ref, 2 * tm, y_ref, ybuf, 0, sem)

    @pl.when(i + 1 < n)
    def _():
        _start_row_gather(nxt_ref, 2 * tm, y_ref, ybuf, 1 - slot, sem)

    def compute(s):
        _wait_row_gather(2 * tm, y_ref, ybuf, s, sem)
        y0 = _rows_from_tiles(ybuf, s, 0, tm)
        y1 = _rows_from_tiles(ybuf, s, tm, tm)
        wt = wt_ref[...]
        xo = x_ref[...] + wt[:, 4:5] * y0 + wt[:, 5:6] * y1
        if final:
            xo = _rms(xo, fg_ref[...])
        o_ref[...] = xo

    for s in range(2):
        @pl.when(slot == s)
        def _(s=s):
            compute(s)


def _combine(dest, x2d, wt, fg, ybuf, *, final):
    t = x2d.shape[0]
    tm = TM_COMB
    nsteps = t // tm
    row = lambda i: (i, 0)
    return pl.pallas_call(
        functools.partial(_comb_kernel, final=final),
        out_shape=jax.ShapeDtypeStruct((t, D_MODEL), F32),
        grid=(nsteps,),
        in_specs=[
            pl.BlockSpec((1, 1, 2 * tm), lambda i: (i, 0, 0), memory_space=pltpu.SMEM),
            pl.BlockSpec((1, 1, 2 * tm), lambda i: (jnp.minimum(i + 1, nsteps - 1), 0, 0),
                         memory_space=pltpu.SMEM),
            pl.BlockSpec((tm, D_MODEL), row),
            pl.BlockSpec((tm, SUBLANES), row),
            pl.BlockSpec((1, D_MODEL), lambda i: (0, 0)),
            pl.BlockSpec(memory_space=pl.ANY),
        ],
        out_specs=pl.BlockSpec((tm, D_MODEL), row),
        scratch_shapes=[pltpu.VMEM((2, 2 * tm * CHUNKS, LANES), F32),
                        pltpu.SemaphoreType.DMA((2,))],
        compiler_params=pltpu.CompilerParams(
            dimension_semantics=("arbitrary",), vmem_limit_bytes=VMEM_LIMIT),
        name="combine",
    )(dest, dest, x2d, wt, fg, ybuf)


def _rope_tables():
    pos = jnp.arange(SEQ, dtype=F32)
    inv = ROPE_THETA ** (-jnp.arange(0, HEAD_DIM, 2, dtype=F32) / HEAD_DIM)
    ang1 = pos[:, None] * inv[None, :]
    rows = SEQ // GRID_W
    r = jnp.broadcast_to(jnp.arange(rows, dtype=F32)[:, None], (rows, GRID_W)).reshape(-1)
    c = jnp.broadcast_to(jnp.arange(GRID_W, dtype=F32)[None, :], (rows, GRID_W)).reshape(-1)
    axis_dim = HEAD_DIM // 2
    inv2 = ROPE_THETA ** (-jnp.arange(0, axis_dim, 2, dtype=F32) / axis_dim)
    ang2 = jnp.concatenate([r[:, None] * inv2, c[:, None] * inv2], axis=-1)

    def tables(ang):
        cs, sn = jnp.cos(ang), jnp.sin(ang)
        reps = LANES // HEAD_DIM
        return (jnp.tile(jnp.concatenate([cs, cs], axis=1), (1, reps)),
                jnp.tile(jnp.concatenate([-sn, sn], axis=1), (1, reps)))

    return tables(ang1) + tables(ang2)


def kernel(x, mem, mem_norm_g, w_mem_kv, norm1_g, w_in, lam_q1, lam_k1, lam_q2, lam_k2,
           subln_g, q_norm_g, k_norm_g, w_up_a, w_up_b, w_up_c, w_out, norm2_g,
           w_router_group, b_router_group, w_router_expert, b_router_expert,
           w_exp_gate, w_exp_up, w_exp_down, final_norm_g):
    batch, seq, d = x.shape
    depth = w_in.shape[0]
    assert (seq, d, mem.shape[1]) == (SEQ, D_MODEL, MEM_LEN)
    t = batch * seq
    ca, sa, cb, sb = _rope_tables()
    gidx = jnp.arange(512) // HEAD_DIM
    bd = (gidx[:, None] == gidx[None, :]).astype(BF16)
    ti = jnp.arange(TM_POST)
    tri = (ti[:, None] < ti[None, :]).astype(BF16)
    nblk = (t * TOP_K) // BM + N_EXPERTS

    mkv = _memkv(mem.reshape(batch * MEM_LEN, d), mem_norm_g.reshape(1, d), w_mem_kv.astype(BF16))
    x2d = x.reshape(t, d)
    ybuf = None
    for l in range(depth):
        wl = w_in[l]
        bk0, bk1 = wl[:, 2048:2112], wl[:, 2112:2176]
        bv0, bv1 = wl[:, 2176:2240], wl[:, 2240:2304]
        w = jnp.concatenate([wl[:, :2048], bk0, bk0, bk1, bk1, bv0, bv0, bv1, bv1, wl[:, 2304:]],
                            axis=1).astype(BF16)
        aq, ak, av, bq, bk, bv, cq, gate = _in_proj(
            x2d, norm1_g[l].reshape(1, d), w, ca, sa, cb, sb,
            jnp.tile(q_norm_g[l], 8).reshape(1, 512), jnp.tile(k_norm_g[l], 4).reshape(1, 256), bd)

        lam_init = 0.8 - 0.6 * math.exp(-0.3 * l)
        lamp = jnp.stack([lam_q1[l], lam_k1[l], lam_q2[l], lam_k2[l]]).astype(F32)
        oa = _attention(aq, ak, av, batch, diff=True, lamp=lamp,
                        gs=subln_g[l].reshape(LANES, 1), lam_init=lam_init)
        ob = _attention(bq, bk, bv, batch, diff=False)

        wr = jnp.zeros((ROUTE_ROWS, d), F32)
        wr = wr.at[0:N_GROUPS].set(w_router_group[l].T).at[SUBLANES:].set(w_router_expert[l].T)
        wr_hi = wr.astype(BF16)
        wr_lo = (wr - wr_hi.astype(F32)).astype(BF16)
        br = jnp.zeros((ROUTE_ROWS, 1), F32)
        br = br.at[0:N_GROUPS, 0].set(b_router_group[l]).at[SUBLANES:, 0].set(b_router_expert[l])
        x2d, h2, route, cnt = _post(
            x2d, oa, ob, cq, gate, mkv, w_up_a[l].astype(BF16), w_up_b[l].astype(BF16),
            w_up_c[l].astype(BF16), w_out[l].astype(BF16), norm2_g[l].reshape(1, d),
            jnp.concatenate([wr_hi, wr_lo], axis=0), br, tri)

        counts = cnt[:, 0].astype(jnp.int32)
        nblk_e = (counts + BM - 1) // BM
        cum = jnp.cumsum(nblk_e)
        first = cum - nblk_e
        nb_used = cum[-1].reshape(1)
        bidx = jnp.minimum(jnp.arange(nblk, dtype=jnp.int32), nb_used - 1)
        bexp = jnp.sum((cum[None, :] <= bidx[:, None]).astype(jnp.int32), axis=1)
        bval = jnp.clip(counts[bexp] - (bidx - first[bexp]) * BM, 0, BM)
        experts = route[0:2].astype(jnp.int32)
        hit = experts[:, :, None] == jnp.arange(N_EXPERTS, dtype=jnp.int32)
        slots = (jnp.sum(jnp.where(hit, first * BM, 0), axis=-1)
                 + route[2:4].astype(jnp.int32))

        if ybuf is None:
            ybuf = jnp.zeros((nblk * BM * CHUNKS, LANES), F32)
        xbuf = _dispatch(slots.T.reshape(t // TD, 1, TOP_K * TD), h2, ybuf)
        ybuf = _moe(bexp, bval, nb_used, xbuf, w_exp_gate, w_exp_up, w_exp_down, l)

        dest_blk = slots.reshape(2, t // TM_COMB, TM_COMB).transpose(1, 0, 2).reshape(
            t // TM_COMB, 1, 2 * TM_COMB)
        x2d = _combine(dest_blk, x2d, route.T, final_norm_g.reshape(1, d), ybuf,
                       final=(l == depth - 1))
    return x2d.reshape(batch, seq, d)
```

```python
import functools
import math

import jax
import jax.numpy as jnp
from jax import lax
from jax.experimental import pallas as pl
from jax.experimental.pallas import tpu as pltpu

F32 = jnp.float32
BF16 = jnp.bfloat16

D_MODEL = 1024
SEQ = 2048
MEM_LEN = 256
HEAD_DIM = 64
MX_HEAD_DIM = 128
BRANCH_W = 512
GRID_W = 64
ROPE_THETA = 10000.0
NORM_EPS = 1e-6
N_GROUPS = 4
EXPERTS_PER_GROUP = 8
N_EXPERTS = N_GROUPS * EXPERTS_PER_GROUP
TOP_K = 2
D_EXPERT = 512

LANES = 128
SUBLANES = 8
CHUNKS = D_MODEL // LANES

TM_IN = 512
TQ = 1024
TK = 256
ONES_ROWS = 16
LOG2E = math.log2(math.e)
TM_POST = 512
BM = 256
TM_COMB = 256
TD = 1024
ISSUE_UNROLL = 8
VMEM_LIMIT = 56 * 1024 * 1024

C_AQ, C_AK, C_AV, C_BQ, C_BK, C_BV, C_CQ, C_G, C_END = (
    0, 512, 1024, 1536, 2048, 2304, 2560, 3072, 6144)
ROUTE_ROWS = 40


def _rms(xf, g):
    ms = jnp.mean(xf * xf, axis=-1, keepdims=True)
    return xf * lax.rsqrt(ms + NORM_EPS) * g


def _dot(a, b):
    return jnp.dot(a, b, preferred_element_type=F32)


def _dot_nt(a, b):
    return lax.dot_general(a, b, (((1,), (1,)), ((), ())), preferred_element_type=F32)


def _memkv_kernel(m_ref, g_ref, w_ref, o_ref):
    h = _rms(m_ref[...], g_ref[...]).astype(BF16)
    o_ref[...] = _dot(h, w_ref[...]).astype(BF16)


def _memkv(mem2d, g, w):
    n = mem2d.shape[0]
    tm = 512
    return pl.pallas_call(
        _memkv_kernel,
        out_shape=jax.ShapeDtypeStruct((n, w.shape[1]), BF16),
        grid=(n // tm,),
        in_specs=[pl.BlockSpec((tm, D_MODEL), lambda i: (i, 0)),
                  pl.BlockSpec((1, D_MODEL), lambda i: (0, 0)),
                  pl.BlockSpec(w.shape, lambda i: (0, 0))],
        out_specs=pl.BlockSpec((tm, w.shape[1]), lambda i: (i, 0)),
        compiler_params=pltpu.CompilerParams(
            dimension_semantics=("arbitrary",), vmem_limit_bytes=VMEM_LIMIT),
        name="memkv",
    )(mem2d, g, w)


def _in_kernel(x_ref, g1_ref, w_ref, ca_ref, sa_ref, cb_ref, sb_ref, qg_ref, kg_ref, bd_ref,
               aq_ref, ak_ref, av_ref, bq_ref, bk_ref, bv_ref, cq_ref, gate_ref):
    tm = x_ref.shape[0]
    h = _rms(x_ref[...], g1_ref[...]).astype(BF16)
    lane = lax.broadcasted_iota(jnp.int32, (tm, LANES), 1)
    first_half = (lane & (HEAD_DIM // 2)) == 0

    def seg(lo, hi):
        return _dot(h, w_ref[:, lo:hi])

    def rope(p, c, s):
        sw = jnp.where(first_half, pltpu.roll(p, LANES - HEAD_DIM // 2, 1),
                       pltpu.roll(p, HEAD_DIM // 2, 1))
        return p * c + sw * s

    def group_norm(p, gain):
        n = p.shape[1]
        ss = _dot((p * p).astype(BF16), bd_ref[:n, :n])
        return p * lax.rsqrt(ss * (1.0 / HEAD_DIM) + NORM_EPS) * gain

    def rope_store(p, c_ref, s_ref, o_ref, scale):
        c = c_ref[...]
        s = s_ref[...]
        for j in range(p.shape[1] // LANES):
            sl = slice(j * LANES, (j + 1) * LANES)
            o_ref[:, sl] = (rope(p[:, sl], c, s) * scale).astype(BF16)

    q_scale = HEAD_DIM ** -0.5 * LOG2E
    rope_store(seg(C_AQ, C_AK), ca_ref, sa_ref, aq_ref, q_scale)
    rope_store(seg(C_AK, C_AV), ca_ref, sa_ref, ak_ref, 1.0)
    av_ref[...] = seg(C_AV, C_BQ).T.astype(BF16)
    rope_store(group_norm(seg(C_BQ, C_BK), qg_ref[...]), cb_ref, sb_ref, bq_ref, q_scale)
    rope_store(group_norm(seg(C_BK, C_BV), kg_ref[...]), cb_ref, sb_ref, bk_ref, 1.0)
    bv_ref[...] = seg(C_BV, C_CQ).T.astype(BF16)
    cq_ref[...] = (seg(C_CQ, C_G) * (MX_HEAD_DIM ** -0.5)).astype(BF16)
    for j in range((C_END - C_G) // 512):
        lo = C_G + j * 512
        z = seg(lo, lo + 512)
        gate_ref[:, j * 512:(j + 1) * 512] = (1.0 / (1.0 + jnp.exp(-z))).astype(BF16)


def _in_proj(x2d, g1, w, ca, sa, cb, sb, qg, kg, bd):
    t = x2d.shape[0]
    tm = TM_IN
    nrb = SEQ // tm
    row = lambda i: (i, 0)
    const = lambda i: (0, 0)
    tab = lambda i: (i % nrb, 0)
    outs = ((512, False), (512, False), (512, True), (512, False), (256, False), (256, True),
            (512, False), (3072, False))
    col = lambda i: (0, i)
    return pl.pallas_call(
        _in_kernel,
        out_shape=[jax.ShapeDtypeStruct((n, t) if tr else (t, n), BF16) for n, tr in outs],
        grid=(t // tm,),
        in_specs=[pl.BlockSpec((tm, D_MODEL), row),
                  pl.BlockSpec((1, D_MODEL), const),
                  pl.BlockSpec(w.shape, const),
                  pl.BlockSpec((tm, LANES), tab), pl.BlockSpec((tm, LANES), tab),
                  pl.BlockSpec((tm, LANES), tab), pl.BlockSpec((tm, LANES), tab),
                  pl.BlockSpec((1, 512), const), pl.BlockSpec((1, 256), const),
                  pl.BlockSpec((512, 512), const)],
        out_specs=[pl.BlockSpec((n, tm), col) if tr else pl.BlockSpec((tm, n), row)
                   for n, tr in outs],
        compiler_params=pltpu.CompilerParams(
            dimension_semantics=("arbitrary",), vmem_limit_bytes=VMEM_LIMIT),
        name="in_proj",
    )(x2d, g1, w, ca, sa, cb, sb, qg, kg, bd)


def _attn_kernel(*refs, diff, post_scale, lam_init):
    if diff:
        lamp_ref, gs_ref, q_ref, k_ref, vt_ref, o_ref = refs
    else:
        q_ref, k_ref, vt_ref, o_ref = refs
    tq = q_ref.shape[0]
    q = q_ref[...]
    lane = lax.broadcasted_iota(jnp.int32, (tq, LANES), 1)
    lo = lane < HEAD_DIM
    zero = jnp.zeros_like(q)
    qs = jnp.concatenate([jnp.where(lo, q, zero), jnp.where(lo, zero, q)], axis=0)
    cols = 2 * tq
    ones = jnp.ones((ONES_ROWS, TK), BF16)
    m = jnp.full((1, cols), -jnp.inf, F32)
    acc = jnp.zeros((LANES + ONES_ROWS, cols), F32)
    nchunks = SEQ // TK

    def scores(j):
        return _dot_nt(k_ref[j * TK:(j + 1) * TK, :], qs)

    st_next = scores(0)
    for j in range(nchunks):
        st = st_next
        if j + 1 < nchunks:
            st_next = scores(j + 1)
        vtj = jnp.concatenate([vt_ref[:, j * TK:(j + 1) * TK], ones], axis=0)
        m_new = jnp.maximum(m, jnp.max(st, axis=0, keepdims=True))
        alpha = jnp.exp2(m - m_new)
        e = jnp.exp2(st - m_new).astype(BF16)
        acc = alpha * acc + _dot(vtj, e)
        m = m_new
    o = acc[:LANES] / acc[LANES:LANES + 1]
    if diff:
        lp = lamp_ref[...]
        lam = (jnp.exp(jnp.sum(lp[0:1] * lp[1:2], axis=-1, keepdims=True))
               - jnp.exp(jnp.sum(lp[2:3] * lp[3:4], axis=-1, keepdims=True)) + lam_init)
        d = o[:, :tq] - lam * o[:, tq:]
        ms = jnp.mean(d * d, axis=0, keepdims=True)
        out_t = d * lax.rsqrt(ms + NORM_EPS) * gs_ref[...] * post_scale
    else:
        row = lax.broadcasted_iota(jnp.int32, (LANES, tq), 0)
        out_t = jnp.where(row < HEAD_DIM, o[:, :tq], o[:, tq:])
    o_ref[...] = out_t.T.astype(BF16)


def _attention(q, k, vt, batch, *, diff, lamp=None, gs=None, lam_init=0.0):
    t = q.shape[0]
    nq = SEQ // TQ
    nblk = q.shape[1] // LANES
    kv_per = nblk // (k.shape[1] // LANES)
    qmap = lambda b, h, i: (b * nq + i, h)
    in_specs = [pl.BlockSpec((TQ, LANES), qmap),
                pl.BlockSpec((SEQ, LANES), lambda b, h, i: (b, h // kv_per)),
                pl.BlockSpec((LANES, SEQ), lambda b, h, i: (h // kv_per, b))]
    args = [q, k, vt]
    if diff:
        const = lambda b, h, i: (0, 0)
        in_specs = [pl.BlockSpec((4, HEAD_DIM), const), pl.BlockSpec((LANES, 1), const)] + in_specs
        args = [lamp, gs] + args
    return pl.pallas_call(
        functools.partial(_attn_kernel, diff=diff, post_scale=1.0 - lam_init, lam_init=lam_init),
        out_shape=jax.ShapeDtypeStruct((t, q.shape[1]), BF16),
        grid=(batch, nblk, nq),
        in_specs=in_specs,
        out_specs=pl.BlockSpec((TQ, LANES), qmap),
        compiler_params=pltpu.CompilerParams(
            dimension_semantics=("arbitrary", "arbitrary", "arbitrary"),
            vmem_limit_bytes=VMEM_LIMIT),
        name="attn_diff" if diff else "attn_gqa",
    )(*args)


def _post_kernel(x_ref, oa_ref, ob_ref, cq_ref, gate_ref, mkv_ref, wa_ref, wb_ref, wc_ref,
                 wo_ref, g2_ref, wr_ref, br_ref, tri_ref,
                 xo_ref, h2_ref, route_ref, cnt_ref, carry_ref):
    tm = x_ref.shape[0]
    i = pl.program_id(0)

    @pl.when(i == 0)
    def _():
        carry_ref[...] = jnp.zeros_like(carry_ref)

    heads = []
    for hd in range(BRANCH_W // MX_HEAD_DIM):
        sl = slice(hd * MX_HEAD_DIM, (hd + 1) * MX_HEAD_DIM)
        sv = slice(BRANCH_W + hd * MX_HEAD_DIM, BRANCH_W + (hd + 1) * MX_HEAD_DIM)
        s = _dot_nt(cq_ref[:, sl], mkv_ref[:, sl])
        e = jnp.exp(s - jnp.max(s, axis=-1, keepdims=True))
        den = jnp.sum(e, axis=-1, keepdims=True)
        heads.append((_dot(e.astype(BF16), mkv_ref[:, sv]) / den).astype(BF16))
    oc = jnp.concatenate(heads, axis=1)

    ya = _dot(oa_ref[...], wa_ref[...])
    yb = _dot(ob_ref[...], wb_ref[...])
    yc = _dot(oc, wc_ref[...])
    merged = (gate_ref[:, 0:D_MODEL].astype(F32) * ya
              + gate_ref[:, D_MODEL:2 * D_MODEL].astype(F32) * yb
              + gate_ref[:, 2 * D_MODEL:3 * D_MODEL].astype(F32) * yc)
    xn = x_ref[...] + _dot(merged.astype(BF16), wo_ref[...])
    xo_ref[...] = xn
    h2 = _rms(xn, g2_ref[...])
    for c in range(CHUNKS):
        h2_ref[pl.ds(c, tm, stride=CHUNKS), :] = h2[:, c * LANES:(c + 1) * LANES]

    h_hi = h2.astype(BF16)
    h_lo = (h2 - h_hi.astype(F32)).astype(BF16)
    l2 = _dot_nt(wr_ref[...], h_hi)
    logits = (l2[:ROUTE_ROWS] + l2[ROUTE_ROWS:] + _dot_nt(wr_ref[:ROUTE_ROWS, :], h_lo)
              + br_ref[...])

    neg = -jnp.inf
    r8 = lax.broadcasted_iota(jnp.int32, (SUBLANES, tm), 0)
    r32 = lax.broadcasted_iota(jnp.int32, (N_EXPERTS, tm), 0)
    gl = jnp.where(r8 < N_GROUPS, logits[0:SUBLANES], neg)
    gmax = jnp.max(gl, axis=0, keepdims=True)
    gidx = jnp.min(jnp.where(gl == gmax, r8, SUBLANES), axis=0, keepdims=True)
    gp = 1.0 / jnp.sum(jnp.exp(gl - gmax), axis=0, keepdims=True)
    el = jnp.where((r32 // EXPERTS_PER_GROUP) == gidx, logits[SUBLANES:ROUTE_ROWS], neg)
    m1 = jnp.max(el, axis=0, keepdims=True)
    i1 = jnp.min(jnp.where(el == m1, r32, N_EXPERTS), axis=0, keepdims=True)
    el2 = jnp.where(r32 == i1, neg, el)
    m2 = jnp.max(el2, axis=0, keepdims=True)
    i2 = jnp.min(jnp.where(el2 == m2, r32, N_EXPERTS), axis=0, keepdims=True)
    d = jnp.exp(m2 - m1)
    w1 = gp / (1.0 + d)
    w2 = gp * d / (1.0 + d)

    hit1 = r32 == i1
    hit2 = r32 == i2
    oh = jnp.where(hit1 | hit2, 1.0, 0.0)
    before = _dot(oh.astype(BF16), tri_ref[...]) + carry_ref[...]
    rank1 = jnp.sum(jnp.where(hit1, before, 0.0), axis=0, keepdims=True)
    rank2 = jnp.sum(jnp.where(hit2, before, 0.0), axis=0, keepdims=True)
    carry_ref[...] = carry_ref[...] + jnp.sum(oh, axis=1, keepdims=True)
    zrow = jnp.zeros_like(w1)
    route_ref[...] = jnp.concatenate(
        [i1.astype(F32), i2.astype(F32), rank1, rank2, w1, w2, zrow, zrow], axis=0)
    cnt_ref[...] = jnp.broadcast_to(carry_ref[...], cnt_ref.shape)


def _post(x2d, oa, ob, cq, gate, mkv, wa, wb, wc, wo, g2, wr, br, tri):
    t = x2d.shape[0]
    tm = TM_POST
    nrb = SEQ // tm
    row = lambda i: (i, 0)
    const = lambda i: (0, 0)
    return pl.pallas_call(
        _post_kernel,
        out_shape=[jax.ShapeDtypeStruct((t, D_MODEL), F32),
                   jax.ShapeDtypeStruct((t * CHUNKS, LANES), F32),
                   jax.ShapeDtypeStruct((SUBLANES, t), F32),
                   jax.ShapeDtypeStruct((N_EXPERTS, LANES), F32)],
        grid=(t // tm,),
        in_specs=[pl.BlockSpec((tm, D_MODEL), row),
                  pl.BlockSpec((tm, BRANCH_W), row),
                  pl.BlockSpec((tm, BRANCH_W), row),
                  pl.BlockSpec((tm, BRANCH_W), row),
                  pl.BlockSpec((tm, 3 * D_MODEL), row),
                  pl.BlockSpec((MEM_LEN, 2 * BRANCH_W), lambda i: (i // nrb, 0)),
                  pl.BlockSpec(wa.shape, const), pl.BlockSpec(wb.shape, const),
                  pl.BlockSpec(wc.shape, const), pl.BlockSpec(wo.shape, const),
                  pl.BlockSpec((1, D_MODEL), const),
                  pl.BlockSpec(wr.shape, const), pl.BlockSpec(br.shape, const),
                  pl.BlockSpec(tri.shape, const)],
        out_specs=[pl.BlockSpec((tm, D_MODEL), row),
                   pl.BlockSpec((tm * CHUNKS, LANES), row),
                   pl.BlockSpec((SUBLANES, tm), lambda i: (0, i)),
                   pl.BlockSpec((N_EXPERTS, LANES), const)],
        scratch_shapes=[pltpu.VMEM((N_EXPERTS, 1), F32)],
        compiler_params=pltpu.CompilerParams(
            dimension_semantics=("arbitrary",), vmem_limit_bytes=VMEM_LIMIT),
        name="post",
    )(x2d, oa, ob, cq, gate, mkv, wa, wb, wc, wo, g2, wr, br, tri)


def _start_row_gather(idx_ref, n, src_ref, dst_ref, slot, sem):
    def body(i, carry):
        for u in range(ISSUE_UNROLL):
            r = i * ISSUE_UNROLL + u
            tok = idx_ref[0, 0, r]
            pltpu.make_async_copy(
                src_ref.at[pl.ds(pl.multiple_of(tok * CHUNKS, CHUNKS), CHUNKS)],
                dst_ref.at[slot, pl.ds(pl.multiple_of(r * CHUNKS, CHUNKS), CHUNKS)],
                sem.at[slot]).start(priority=u % 2)
        return carry
    lax.fori_loop(0, n // ISSUE_UNROLL, body, 0)


def _wait_row_gather(n, src_ref, dst_ref, slot, sem):
    pltpu.make_async_copy(src_ref.at[pl.ds(0, n * CHUNKS)], dst_ref.at[slot], sem.at[slot]).wait()


def _rows_from_tiles(buf_ref, slot, first, n):
    return jnp.concatenate(
        [buf_ref[slot, pl.ds(first * CHUNKS + c, n, stride=CHUNKS), :] for c in range(CHUNKS)],
        axis=1)


def _dispatch_kernel(idx_ref, h2_ref, xin_ref, xbuf_ref, sem):
    del xin_ref
    n = idx_ref.shape[2]

    def body(i, carry):
        for u in range(ISSUE_UNROLL):
            r = i * ISSUE_UNROLL + u
            slot_row = idx_ref[0, 0, r]
            pltpu.make_async_copy(
                h2_ref.at[pl.ds(pl.multiple_of((r // TOP_K) * CHUNKS, CHUNKS), CHUNKS)],
                xbuf_ref.at[pl.ds(pl.multiple_of(slot_row * CHUNKS, CHUNKS), CHUNKS)],
                sem.at[0]).start(priority=u % 2)
        return carry
    lax.fori_loop(0, n // ISSUE_UNROLL, body, 0)
    for _ in range(TOP_K):
        pltpu.make_async_copy(h2_ref, xbuf_ref.at[pl.ds(0, h2_ref.shape[0])], sem.at[0]).wait()


def _dispatch(slots, h2, xbuf_init):
    nsteps, _, n = slots.shape
    td = n // TOP_K
    return pl.pallas_call(
        _dispatch_kernel,
        out_shape=jax.ShapeDtypeStruct(xbuf_init.shape, F32),
        grid=(nsteps,),
        in_specs=[pl.BlockSpec((1, 1, n), lambda i: (i, 0, 0), memory_space=pltpu.SMEM),
                  pl.BlockSpec((td * CHUNKS, LANES), lambda i: (i, 0)),
                  pl.BlockSpec(memory_space=pl.ANY)],
        out_specs=pl.BlockSpec(memory_space=pl.ANY),
        input_output_aliases={2: 0},
        scratch_shapes=[pltpu.SemaphoreType.DMA((1,))],
        compiler_params=pltpu.CompilerParams(
            dimension_semantics=("arbitrary",), vmem_limit_bytes=VMEM_LIMIT),
        name="dispatch",
    )(slots, h2, xbuf_init)


def _moe_kernel(bexp_ref, bval_ref, nb_ref, x_ref, wg_ref, wu_ref, wd_ref, y_ref,
                wgb, wub, wdb):
    b = pl.program_id(0)

    @pl.when((b == 0) | (bexp_ref[b] != bexp_ref[jnp.maximum(b - 1, 0)]))
    def _():
        wgb[...] = wg_ref[0, 0].astype(BF16)
        wub[...] = wu_ref[0, 0].astype(BF16)
        wdb[...] = wd_ref[0, 0].astype(BF16)

    @pl.when(b >= nb_ref[0])
    def _():
        y_ref[...] = jnp.zeros_like(y_ref)

    @pl.when(b < nb_ref[0])
    def _():
        xf = jnp.concatenate(
            [x_ref[pl.ds(c, BM, stride=CHUNKS), :] for c in range(CHUNKS)], axis=1)
        rowid = lax.broadcasted_iota(jnp.int32, (BM, 1), 0)
        xb = jnp.where(rowid < bval_ref[b], xf, 0.0).astype(BF16)
        hg = _dot(xb, wgb[...])
        hu = _dot(xb, wub[...])
        hid = (hg / (1.0 + jnp.exp(-hg)) * hu).astype(BF16)
        y = _dot(hid, wdb[...])
        for c in range(CHUNKS):
            y_ref[pl.ds(c, BM, stride=CHUNKS), :] = y[:, c * LANES:(c + 1) * LANES]


def _moe(bexp, bval, nb_used, xbuf, wg, wu, wd, layer):
    nblk = bexp.shape[0]
    wmap = lambda b, bexp, bval, nb: (layer, bexp[b], 0, 0)
    xmap = lambda b, bexp, bval, nb: (jnp.minimum(b, nb[0] - 1), 0)
    return pl.pallas_call(
        _moe_kernel,
        out_shape=jax.ShapeDtypeStruct(xbuf.shape, F32),
        grid_spec=pltpu.PrefetchScalarGridSpec(
            num_scalar_prefetch=3,
            grid=(nblk,),
            in_specs=[
                pl.BlockSpec((BM * CHUNKS, LANES), xmap),
                pl.BlockSpec((1, 1, D_MODEL, D_EXPERT), wmap),
                pl.BlockSpec((1, 1, D_MODEL, D_EXPERT), wmap),
                pl.BlockSpec((1, 1, D_EXPERT, D_MODEL), wmap),
            ],
            out_specs=pl.BlockSpec((BM * CHUNKS, LANES), lambda b, bexp, bval, nb: (b, 0)),
            scratch_shapes=[pltpu.VMEM((D_MODEL, D_EXPERT), BF16),
                            pltpu.VMEM((D_MODEL, D_EXPERT), BF16),
                            pltpu.VMEM((D_EXPERT, D_MODEL), BF16)]),
        compiler_params=pltpu.CompilerParams(
            dimension_semantics=("arbitrary",), vmem_limit_bytes=VMEM_LIMIT),
        name="experts",
    )(bexp, bval, nb_used, xbuf, wg, wu, wd)


def _comb_kernel(cur_ref, nxt_ref, x_ref, wt_ref, fg_ref, y_ref, o_ref, ybuf, sem, *, final):
    tm = x_ref.shape[0]
    i = pl.program_id(0)
    n = pl.num_programs(0)
    slot = i % 2

    @pl.when(i == 0)
    def _():
        _start_row_gather(cur_ref, 2 * tm, y_ref, ybuf, 0, sem)

    @pl.when(i + 1 < n)
    def _():
        _start_row_gather(nxt_ref, 2 * tm, y_ref, ybuf, 1 - slot, sem)

    def compute(s):
        _wait_row_gather(2 * tm, y_ref, ybuf, s, sem)
        y0 = _rows_from_tiles(ybuf, s, 0, tm)
        y1 = _rows_from_tiles(ybuf, s, tm, tm)
        wt = wt_ref[...]
        xo = x_ref[...] + wt[:, 4:5] * y0 + wt[:, 5:6] * y1
        if final:
            xo = _rms(xo, fg_ref[...])
        o_ref[...] = xo

    for s in range(2):
        @pl.when(slot == s)
        def _(s=s):
            compute(s)


def _combine(dest, x2d, wt, fg, ybuf, *, final):
    t = x2d.shape[0]
    tm = TM_COMB
    nsteps = t // tm
    row = lambda i: (i, 0)
    return pl.pallas_call(
        functools.partial(_comb_kernel, final=final),
        out_shape=jax.ShapeDtypeStruct((t, D_MODEL), F32),
        grid=(nsteps,),
        in_specs=[
            pl.BlockSpec((1, 1, 2 * tm), lambda i: (i, 0, 0), memory_space=pltpu.SMEM),
            pl.BlockSpec((1, 1, 2 * tm), lambda i: (jnp.minimum(i + 1, nsteps - 1), 0, 0),
                         memory_space=pltpu.SMEM),
            pl.BlockSpec((tm, D_MODEL), row),
            pl.BlockSpec((tm, SUBLANES), row),
            pl.BlockSpec((1, D_MODEL), lambda i: (0, 0)),
            pl.BlockSpec(memory_space=pl.ANY),
        ],
        out_specs=pl.BlockSpec((tm, D_MODEL), row),
        scratch_shapes=[pltpu.VMEM((2, 2 * tm * CHUNKS, LANES), F32),
                        pltpu.SemaphoreType.DMA((2,))],
        compiler_params=pltpu.CompilerParams(
            dimension_semantics=("arbitrary",), vmem_limit_bytes=VMEM_LIMIT),
        name="combine",
    )(dest, dest, x2d, wt, fg, ybuf)


def _rope_tables():
    pos = jnp.arange(SEQ, dtype=F32)
    inv = ROPE_THETA ** (-jnp.arange(0, HEAD_DIM, 2, dtype=F32) / HEAD_DIM)
    ang1 = pos[:, None] * inv[None, :]
    rows = SEQ // GRID_W
    r = jnp.broadcast_to(jnp.arange(rows, dtype=F32)[:, None], (rows, GRID_W)).reshape(-1)
    c = jnp.broadcast_to(jnp.arange(GRID_W, dtype=F32)[None, :], (rows, GRID_W)).reshape(-1)
    axis_dim = HEAD_DIM // 2
    inv2 = ROPE_THETA ** (-jnp.arange(0, axis_dim, 2, dtype=F32) / axis_dim)
    ang2 = jnp.concatenate([r[:, None] * inv2, c[:, None] * inv2], axis=-1)

    def tables(ang):
        cs, sn = jnp.cos(ang), jnp.sin(ang)
        reps = LANES // HEAD_DIM
        return (jnp.tile(jnp.concatenate([cs, cs], axis=1), (1, reps)),
                jnp.tile(jnp.concatenate([-sn, sn], axis=1), (1, reps)))

    return tables(ang1) + tables(ang2)


def kernel(x, mem, mem_norm_g, w_mem_kv, norm1_g, w_in, lam_q1, lam_k1, lam_q2, lam_k2,
           subln_g, q_norm_g, k_norm_g, w_up_a, w_up_b, w_up_c, w_out, norm2_g,
           w_router_group, b_router_group, w_router_expert, b_router_expert,
           w_exp_gate, w_exp_up, w_exp_down, final_norm_g):
    batch, seq, d = x.shape
    depth = w_in.shape[0]
    assert (seq, d, mem.shape[1]) == (SEQ, D_MODEL, MEM_LEN)
    t = batch * seq
    ca, sa, cb, sb = _rope_tables()
    gidx = jnp.arange(512) // HEAD_DIM
    bd = (gidx[:, None] == gidx[None, :]).astype(BF16)
    ti = jnp.arange(TM_POST)
    tri = (ti[:, None] < ti[None, :]).astype(BF16)
    nblk = (t * TOP_K) // BM + N_EXPERTS

    mkv = _memkv(mem.reshape(batch * MEM_LEN, d), mem_norm_g.reshape(1, d), w_mem_kv.astype(BF16))
    x2d = x.reshape(t, d)
    ybuf = None
    for l in range(depth):
        wl = w_in[l]
        bk0, bk1 = wl[:, 2048:2112], wl[:, 2112:2176]
        bv0, bv1 = wl[:, 2176:2240], wl[:, 2240:2304]
        w = jnp.concatenate([wl[:, :2048], bk0, bk0, bk1, bk1, bv0, bv0, bv1, bv1, wl[:, 2304:]],
                            axis=1).astype(BF16)
        aq, ak, av, bq, bk, bv, cq, gate = _in_proj(
            x2d, norm1_g[l].reshape(1, d), w, ca, sa, cb, sb,
            jnp.tile(q_norm_g[l], 8).reshape(1, 512), jnp.tile(k_norm_g[l], 4).reshape(1, 256), bd)

        lam_init = 0.8 - 0.6 * math.exp(-0.3 * l)
        lamp = jnp.stack([lam_q1[l], lam_k1[l], lam_q2[l], lam_k2[l]]).astype(F32)
        oa = _attention(aq, ak, av, batch, diff=True, lamp=lamp,
                        gs=subln_g[l].reshape(LANES, 1), lam_init=lam_init)
        ob = _attention(bq, bk, bv, batch, diff=False)

        wr = jnp.zeros((ROUTE_ROWS, d), F32)
        wr = wr.at[0:N_GROUPS].set(w_router_group[l].T).at[SUBLANES:].set(w_router_expert[l].T)
        wr_hi = wr.astype(BF16)
        wr_lo = (wr - wr_hi.astype(F32)).astype(BF16)
        br = jnp.zeros((ROUTE_ROWS, 1), F32)
        br = br.at[0:N_GROUPS, 0].set(b_router_group[l]).at[SUBLANES:, 0].set(b_router_expert[l])
        x2d, h2, route, cnt = _post(
            x2d, oa, ob, cq, gate, mkv, w_up_a[l].astype(BF16), w_up_b[l].astype(BF16),
            w_up_c[l].astype(BF16), w_out[l].astype(BF16), norm2_g[l].reshape(1, d),
            jnp.concatenate([wr_hi, wr_lo], axis=0), br, tri)

        counts = cnt[:, 0].astype(jnp.int32)
        nblk_e = (counts + BM - 1) // BM
        cum = jnp.cumsum(nblk_e)
        first = cum - nblk_e
        nb_used = cum[-1].reshape(1)
        bidx = jnp.minimum(jnp.arange(nblk, dtype=jnp.int32), nb_used - 1)
        bexp = jnp.sum((cum[None, :] <= bidx[:, None]).astype(jnp.int32), axis=1)
        bval = jnp.clip(counts[bexp] - (bidx - first[bexp]) * BM, 0, BM)
        experts = route[0:2].astype(jnp.int32)
        hit = experts[:, :, None] == jnp.arange(N_EXPERTS, dtype=jnp.int32)
        slots = (jnp.sum(jnp.where(hit, first * BM, 0), axis=-1)
                 + route[2:4].astype(jnp.int32))

        if ybuf is None:
            ybuf = jnp.zeros((nblk * BM * CHUNKS, LANES), F32)
        xbuf = _dispatch(slots.T.reshape(t // TD, 1, TOP_K * TD), h2, ybuf)
        ybuf = _moe(bexp, bval, nb_used, xbuf, w_exp_gate, w_exp_up, w_exp_down, l)

        dest_blk = slots.reshape(2, t // TM_COMB, TM_COMB).transpose(1, 0, 2).reshape(
            t // TM_COMB, 1, 2 * TM_COMB)
        x2d = _combine(dest_blk, x2d, route.T, final_norm_g.reshape(1, d), ybuf,
                       final=(l == depth - 1))
    return x2d.reshape(batch, seq, d)
```

```python
import functools
import math

import jax
import jax.numpy as jnp
from jax import lax
from jax.experimental import pallas as pl
from jax.experimental.pallas import tpu as pltpu

F32 = jnp.float32
BF16 = jnp.bfloat16

D_MODEL = 1024
SEQ = 2048
MEM_LEN = 256
HEAD_DIM = 64
MX_HEAD_DIM = 128
BRANCH_W = 512
GRID_W = 64
ROPE_THETA = 10000.0
NORM_EPS = 1e-6
N_GROUPS = 4
EXPERTS_PER_GROUP = 8
N_EXPERTS = N_GROUPS * EXPERTS_PER_GROUP
TOP_K = 2
D_EXPERT = 512

LANES = 128
SUBLANES = 8
CHUNKS = D_MODEL // LANES

TM_IN = 512
TQ = 1024
TK = 256
ONES_ROWS = 16
LOG2E = math.log2(math.e)
TM_POST = 512
BM = 256
TM_COMB = 256
TD = 1024
ISSUE_UNROLL = 8
VMEM_LIMIT = 56 * 1024 * 1024

C_AQ, C_AK, C_AV, C_BQ, C_BK, C_BV, C_CQ, C_G, C_END = (
    0, 512, 1024, 1536, 2048, 2304, 2560, 3072, 6144)
ROUTE_ROWS = 40


def _rms(xf, g):
    ms = jnp.mean(xf * xf, axis=-1, keepdims=True)
    return xf * lax.rsqrt(ms + NORM_EPS) * g


def _dot(a, b):
    return jnp.dot(a, b, preferred_element_type=F32)


def _dot_nt(a, b):
    return lax.dot_general(a, b, (((1,), (1,)), ((), ())), preferred_element_type=F32)


def _memkv_kernel(m_ref, g_ref, w_ref, o_ref):
    h = _rms(m_ref[...], g_ref[...]).astype(BF16)
    o_ref[...] = _dot(h, w_ref[...]).astype(BF16)


def _memkv(mem2d, g, w):
    n = mem2d.shape[0]
    tm = 512
    return pl.pallas_call(
        _memkv_kernel,
        out_shape=jax.ShapeDtypeStruct((n, w.shape[1]), BF16),
        grid=(n // tm,),
        in_specs=[pl.BlockSpec((tm, D_MODEL), lambda i: (i, 0)),
                  pl.BlockSpec((1, D_MODEL), lambda i: (0, 0)),
                  pl.BlockSpec(w.shape, lambda i: (0, 0))],
        out_specs=pl.BlockSpec((tm, w.shape[1]), lambda i: (i, 0)),
        compiler_params=pltpu.CompilerParams(
            dimension_semantics=("arbitrary",), vmem_limit_bytes=VMEM_LIMIT),
        name="memkv",
    )(mem2d, g, w)


def _in_kernel(x_ref, g1_ref, w_ref, ca_ref, sa_ref, cb_ref, sb_ref, qg_ref, kg_ref, bd_ref,
               aq_ref, ak_ref, av_ref, bq_ref, bk_ref, bv_ref, cq_ref, gate_ref):
    tm = x_ref.shape[0]
    h = _rms(x_ref[...], g1_ref[...]).astype(BF16)
    lane = lax.broadcasted_iota(jnp.int32, (tm, LANES), 1)
    first_half = (lane & (HEAD_DIM // 2)) == 0

    def seg(lo, hi):
        return _dot(h, w_ref[:, lo:hi])

    def rope(p, c, s):
        sw = jnp.where(first_half, pltpu.roll(p, LANES - HEAD_DIM // 2, 1),
                       pltpu.roll(p, HEAD_DIM // 2, 1))
        return p * c + sw * s

    def group_norm(p, gain):
        n = p.shape[1]
        ss = _dot((p * p).astype(BF16), bd_ref[:n, :n])
        return p * lax.rsqrt(ss * (1.0 / HEAD_DIM) + NORM_EPS) * gain

    def rope_store(p, c_ref, s_ref, o_ref, scale):
        c = c_ref[...]
        s = s_ref[...]
        for j in range(p.shape[1] // LANES):
            sl = slice(j * LANES, (j + 1) * LANES)
            o_ref[:, sl] = (rope(p[:, sl], c, s) * scale).astype(BF16)

    q_scale = HEAD_DIM ** -0.5 * LOG2E
    rope_store(seg(C_AQ, C_AK), ca_ref, sa_ref, aq_ref, q_scale)
    rope_store(seg(C_AK, C_AV), ca_ref, sa_ref, ak_ref, 1.0)
    av_ref[...] = seg(C_AV, C_BQ).T.astype(BF16)
    rope_store(group_norm(seg(C_BQ, C_BK), qg_ref[...]), cb_ref, sb_ref, bq_ref, q_scale)
    rope_store(group_norm(seg(C_BK, C_BV), kg_ref[...]), cb_ref, sb_ref, bk_ref, 1.0)
    bv_ref[...] = seg(C_BV, C_CQ).T.astype(BF16)
    cq_ref[...] = (seg(C_CQ, C_G) * (MX_HEAD_DIM ** -0.5)).astype(BF16)
    for j in range((C_END - C_G) // 512):
        lo = C_G + j * 512
        z = seg(lo, lo + 512)
        gate_ref[:, j * 512:(j + 1) * 512] = (1.0 / (1.0 + jnp.exp(-z))).astype(BF16)


def _in_proj(x2d, g1, w, ca, sa, cb, sb, qg, kg, bd):
    t = x2d.shape[0]
    tm = TM_IN
    nrb = SEQ // tm
    row = lambda i: (i, 0)
    const = lambda i: (0, 0)
    tab = lambda i: (i % nrb, 0)
    outs = ((512, False), (512, False), (512, True), (512, False), (256, False), (256, True),
            (512, False), (3072, False))
    col = lambda i: (0, i)
    return pl.pallas_call(
        _in_kernel,
        out_shape=[jax.ShapeDtypeStruct((n, t) if tr else (t, n), BF16) for n, tr in outs],
        grid=(t // tm,),
        in_specs=[pl.BlockSpec((tm, D_MODEL), row),
                  pl.BlockSpec((1, D_MODEL), const),
                  pl.BlockSpec(w.shape, const),
                  pl.BlockSpec((tm, LANES), tab), pl.BlockSpec((tm, LANES), tab),
                  pl.BlockSpec((tm, LANES), tab), pl.BlockSpec((tm, LANES), tab),
                  pl.BlockSpec((1, 512), const), pl.BlockSpec((1, 256), const),
                  pl.BlockSpec((512, 512), const)],
        out_specs=[pl.BlockSpec((n, tm), col) if tr else pl.BlockSpec((tm, n), row)
                   for n, tr in outs],
        compiler_params=pltpu.CompilerParams(
            dimension_semantics=("arbitrary",), vmem_limit_bytes=VMEM_LIMIT),
        name="in_proj",
    )(x2d, g1, w, ca, sa, cb, sb, qg, kg, bd)


def _attn_kernel(*refs, diff, post_scale, lam_init):
    if diff:
        lamp_ref, gs_ref, q_ref, k_ref, vt_ref, o_ref = refs
    else:
        q_ref, k_ref, vt_ref, o_ref = refs
    tq = q_ref.shape[0]
    q = q_ref[...]
    lane = lax.broadcasted_iota(jnp.int32, (tq, LANES), 1)
    lo = lane < HEAD_DIM
    zero = jnp.zeros_like(q)
    qs = jnp.concatenate([jnp.where(lo, q, zero), jnp.where(lo, zero, q)], axis=0)
    cols = 2 * tq
    ones = jnp.ones((ONES_ROWS, TK), BF16)
    m = jnp.full((1, cols), -jnp.inf, F32)
    acc = jnp.zeros((LANES + ONES_ROWS, cols), F32)
    nchunks = SEQ // TK

    def scores(j):
        return _dot_nt(k_ref[j * TK:(j + 1) * TK, :], qs)

    st_next = scores(0)
    for j in range(nchunks):
        st = st_next
        if j + 1 < nchunks:
            st_next = scores(j + 1)
        vtj = jnp.concatenate([vt_ref[:, j * TK:(j + 1) * TK], ones], axis=0)
        m_new = jnp.maximum(m, jnp.max(st, axis=0, keepdims=True))
        alpha = jnp.exp2(m - m_new)
        e = jnp.exp2(st - m_new).astype(BF16)
        acc = alpha * acc + _dot(vtj, e)
        m = m_new
    o = acc[:LANES] / acc[LANES:LANES + 1]
    if diff:
        lp = lamp_ref[...]
        lam = (jnp.exp(jnp.sum(lp[0:1] * lp[1:2], axis=-1, keepdims=True))
               - jnp.exp(jnp.sum(lp[2:3] * lp[3:4], axis=-1, keepdims=True)) + lam_init)
        d = o[:, :tq] - lam * o[:, tq:]
        ms = jnp.mean(d * d, axis=0, keepdims=True)
        out_t = d * lax.rsqrt(ms + NORM_EPS) * gs_ref[...] * post_scale
    else:
        row = lax.broadcasted_iota(jnp.int32, (LANES, tq), 0)
        out_t = jnp.where(row < HEAD_DIM, o[:, :tq], o[:, tq:])
    o_ref[...] = out_t.T.astype(BF16)


def _attention(q, k, vt, batch, *, diff, lamp=None, gs=None, lam_init=0.0):
    t = q.shape[0]
    nq = SEQ // TQ
    nblk = q.shape[1] // LANES
    kv_per = nblk // (k.shape[1] // LANES)
    qmap = lambda b, h, i: (b * nq + i, h)
    in_specs = [pl.BlockSpec((TQ, LANES), qmap),
                pl.BlockSpec((SEQ, LANES), lambda b, h, i: (b, h // kv_per)),
                pl.BlockSpec((LANES, SEQ), lambda b, h, i: (h // kv_per, b))]
    args = [q, k, vt]
    if diff:
        const = lambda b, h, i: (0, 0)
        in_specs = [pl.BlockSpec((4, HEAD_DIM), const), pl.BlockSpec((LANES, 1), const)] + in_specs
        args = [lamp, gs] + args
    return pl.pallas_call(
        functools.partial(_attn_kernel, diff=diff, post_scale=1.0 - lam_init, lam_init=lam_init),
        out_shape=jax.ShapeDtypeStruct((t, q.shape[1]), BF16),
        grid=(batch, nblk, nq),
        in_specs=in_specs,
        out_specs=pl.BlockSpec((TQ, LANES), qmap),
        compiler_params=pltpu.CompilerParams(
            dimension_semantics=("arbitrary", "arbitrary", "arbitrary"),
            vmem_limit_bytes=VMEM_LIMIT),
        name="attn_diff" if diff else "attn_gqa",
    )(*args)


def _post_kernel(x_ref, oa_ref, ob_ref, cq_ref, gate_ref, mkv_ref, wa_ref, wb_ref, wc_ref,
                 wo_ref, g2_ref, wr_ref, br_ref, tri_ref,
                 xo_ref, h2_ref, route_ref, cnt_ref, carry_ref):
    tm = x_ref.shape[0]
    i = pl.program_id(0)

    @pl.when(i == 0)
    def _():
        carry_ref[...] = jnp.zeros_like(carry_ref)

    heads = []
    for hd in range(BRANCH_W // MX_HEAD_DIM):
        sl = slice(hd * MX_HEAD_DIM, (hd + 1) * MX_HEAD_DIM)
        sv = slice(BRANCH_W + hd * MX_HEAD_DIM, BRANCH_W + (hd + 1) * MX_HEAD_DIM)
        s = _dot_nt(cq_ref[:, sl], mkv_ref[:, sl])
        e = jnp.exp(s - jnp.max(s, axis=-1, keepdims=True))
        den = jnp.sum(e, axis=-1, keepdims=True)
        heads.append((_dot(e.astype(BF16), mkv_ref[:, sv]) / den).astype(BF16))
    oc = jnp.concatenate(heads, axis=1)

    ya = _dot(oa_ref[...], wa_ref[...])
    yb = _dot(ob_ref[...], wb_ref[...])
    yc = _dot(oc, wc_ref[...])
    merged = (gate_ref[:, 0:D_MODEL].astype(F32) * ya
              + gate_ref[:, D_MODEL:2 * D_MODEL].astype(F32) * yb
              + gate_ref[:, 2 * D_MODEL:3 * D_MODEL].astype(F32) * yc)
    xn = x_ref[...] + _dot(merged.astype(BF16), wo_ref[...])
    xo_ref[...] = xn
    h2 = _rms(xn, g2_ref[...])
    for c in range(CHUNKS):
        h2_ref[pl.ds(c, tm, stride=CHUNKS), :] = h2[:, c * LANES:(c + 1) * LANES]

    h_hi = h2.astype(BF16)
    h_lo = (h2 - h_hi.astype(F32)).astype(BF16)
    l2 = _dot_nt(wr_ref[...], h_hi)
    logits = (l2[:ROUTE_ROWS] + l2[ROUTE_ROWS:] + _dot_nt(wr_ref[:ROUTE_ROWS, :], h_lo)
              + br_ref[...])

    neg = -jnp.inf
    r8 = lax.broadcasted_iota(jnp.int32, (SUBLANES, tm), 0)
    r32 = lax.broadcasted_iota(jnp.int32, (N_EXPERTS, tm), 0)
    gl = jnp.where(r8 < N_GROUPS, logits[0:SUBLANES], neg)
    gmax = jnp.max(gl, axis=0, keepdims=True)
    gidx = jnp.min(jnp.where(gl == gmax, r8, SUBLANES), axis=0, keepdims=True)
    gp = 1.0 / jnp.sum(jnp.exp(gl - gmax), axis=0, keepdims=True)
    el = jnp.where((r32 // EXPERTS_PER_GROUP) == gidx, logits[SUBLANES:ROUTE_ROWS], neg)
    m1 = jnp.max(el, axis=0, keepdims=True)
    i1 = jnp.min(jnp.where(el == m1, r32, N_EXPERTS), axis=0, keepdims=True)
    el2 = jnp.where(r32 == i1, neg, el)
    m2 = jnp.max(el2, axis=0, keepdims=True)
    i2 = jnp.min(jnp.where(el2 == m2, r32, N_EXPERTS), axis=0, keepdims=True)
    d = jnp.exp(m2 - m1)
    w1 = gp / (1.0 + d)
    w2 = gp * d / (1.0 + d)

    hit1 = r32 == i1
    hit2 = r32 == i2
    oh = jnp.where(hit1 | hit2, 1.0, 0.0)
    before = _dot(oh.astype(BF16), tri_ref[...]) + carry_ref[...]
    rank1 = jnp.sum(jnp.where(hit1, before, 0.0), axis=0, keepdims=True)
    rank2 = jnp.sum(jnp.where(hit2, before, 0.0), axis=0, keepdims=True)
    carry_ref[...] = carry_ref[...] + jnp.sum(oh, axis=1, keepdims=True)
    zrow = jnp.zeros_like(w1)
    route_ref[...] = jnp.concatenate(
        [i1.astype(F32), i2.astype(F32), rank1, rank2, w1, w2, zrow, zrow], axis=0)
    cnt_ref[...] = jnp.broadcast_to(carry_ref[...], cnt_ref.shape)


def _post(x2d, oa, ob, cq, gate, mkv, wa, wb, wc, wo, g2, wr, br, tri):
    t = x2d.shape[0]
    tm = TM_POST
    nrb = SEQ // tm
    row = lambda i: (i, 0)
    const = lambda i: (0, 0)
    return pl.pallas_call(
        _post_kernel,
        out_shape=[jax.ShapeDtypeStruct((t, D_MODEL), F32),
                   jax.ShapeDtypeStruct((t * CHUNKS, LANES), F32),
                   jax.ShapeDtypeStruct((SUBLANES, t), F32),
                   jax.ShapeDtypeStruct((N_EXPERTS, LANES), F32)],
        grid=(t // tm,),
        in_specs=[pl.BlockSpec((tm, D_MODEL), row),
                  pl.BlockSpec((tm, BRANCH_W), row),
                  pl.BlockSpec((tm, BRANCH_W), row),
                  pl.BlockSpec((tm, BRANCH_W), row),
                  pl.BlockSpec((tm, 3 * D_MODEL), row),
                  pl.BlockSpec((MEM_LEN, 2 * BRANCH_W), lambda i: (i // nrb, 0)),
                  pl.BlockSpec(wa.shape, const), pl.BlockSpec(wb.shape, const),
                  pl.BlockSpec(wc.shape, const), pl.BlockSpec(wo.shape, const),
                  pl.BlockSpec((1, D_MODEL), const),
                  pl.BlockSpec(wr.shape, const), pl.BlockSpec(br.shape, const),
                  pl.BlockSpec(tri.shape, const)],
        out_specs=[pl.BlockSpec((tm, D_MODEL), row),
                   pl.BlockSpec((tm * CHUNKS, LANES), row),
                   pl.BlockSpec((SUBLANES, tm), lambda i: (0, i)),
                   pl.BlockSpec((N_EXPERTS, LANES), const)],
        scratch_shapes=[pltpu.VMEM((N_EXPERTS, 1), F32)],
        compiler_params=pltpu.CompilerParams(
            dimension_semantics=("arbitrary",), vmem_limit_bytes=VMEM_LIMIT),
        name="post",
    )(x2d, oa, ob, cq, gate, mkv, wa, wb, wc, wo, g2, wr, br, tri)


def _start_row_gather(idx_ref, n, src_ref, dst_ref, slot, sem):
    def body(i, carry):
        for u in range(ISSUE_UNROLL):
            r = i * ISSUE_UNROLL + u
            tok = idx_ref[0, 0, r]
            pltpu.make_async_copy(
                src_ref.at[pl.ds(pl.multiple_of(tok * CHUNKS, CHUNKS), CHUNKS)],
                dst_ref.at[slot, pl.ds(pl.multiple_of(r * CHUNKS, CHUNKS), CHUNKS)],
                sem.at[slot]).start(priority=u % 2)
        return carry
    lax.fori_loop(0, n // ISSUE_UNROLL, body, 0)


def _wait_row_gather(n, src_ref, dst_ref, slot, sem):
    pltpu.make_async_copy(src_ref.at[pl.ds(0, n * CHUNKS)], dst_ref.at[slot], sem.at[slot]).wait()


def _rows_from_tiles(buf_ref, slot, first, n):
    return jnp.concatenate(
        [buf_ref[slot, pl.ds(first * CHUNKS + c, n, stride=CHUNKS), :] for c in range(CHUNKS)],
        axis=1)


def _dispatch_kernel(idx_ref, h2_ref, xin_ref, xbuf_ref, sem, *, alt):
    del xin_ref
    n = idx_ref.shape[2]

    def body(i, carry):
        for u in range(ISSUE_UNROLL):
            r = i * ISSUE_UNROLL + u
            slot_row = idx_ref[0, 0, r]
            pltpu.make_async_copy(
                h2_ref.at[pl.ds(pl.multiple_of((r // TOP_K) * CHUNKS, CHUNKS), CHUNKS)],
                xbuf_ref.at[pl.ds(pl.multiple_of(slot_row * CHUNKS, CHUNKS), CHUNKS)],
                sem.at[0]).start(priority=(u % 2) if alt else 0)
        return carry
    lax.fori_loop(0, n // ISSUE_UNROLL, body, 0)
    for _ in range(TOP_K):
        pltpu.make_async_copy(h2_ref, xbuf_ref.at[pl.ds(0, h2_ref.shape[0])], sem.at[0]).wait()


def _dispatch(slots, h2, xbuf_init, alt):
    nsteps, _, n = slots.shape
    td = n // TOP_K
    return pl.pallas_call(
        functools.partial(_dispatch_kernel, alt=alt),
        out_shape=jax.ShapeDtypeStruct(xbuf_init.shape, F32),
        grid=(nsteps,),
        in_specs=[pl.BlockSpec((1, 1, n), lambda i: (i, 0, 0), memory_space=pltpu.SMEM),
                  pl.BlockSpec((td * CHUNKS, LANES), lambda i: (i, 0)),
                  pl.BlockSpec(memory_space=pl.ANY)],
        out_specs=pl.BlockSpec(memory_space=pl.ANY),
        input_output_aliases={2: 0},
        scratch_shapes=[pltpu.SemaphoreType.DMA((1,))],
        compiler_params=pltpu.CompilerParams(
            dimension_semantics=("arbitrary",), vmem_limit_bytes=VMEM_LIMIT),
        name="dispatch",
    )(slots, h2, xbuf_init)


def _moe_kernel(bexp_ref, bval_ref, nb_ref, x_ref, wg_ref, wu_ref, wd_ref, y_ref,
                wgb, wub, wdb):
    b = pl.program_id(0)

    @pl.when((b == 0) | (bexp_ref[b] != bexp_ref[jnp.maximum(b - 1, 0)]))
    def _():
        wgb[...] = wg_ref[0, 0].astype(BF16)
        wub[...] = wu_ref[0, 0].astype(BF16)
        wdb[...] = wd_ref[0, 0].astype(BF16)

    @pl.when(b >= nb_ref[0])
    def _():
        y_ref[...] = jnp.zeros_like(y_ref)

    @pl.when(b < nb_ref[0])
    def _():
        bm = x_ref.shape[0] // CHUNKS
        xf = jnp.concatenate(
            [x_ref[pl.ds(c, bm, stride=CHUNKS), :] for c in range(CHUNKS)], axis=1)
        rowid = lax.broadcasted_iota(jnp.int32, (bm, 1), 0)
        xb = jnp.where(rowid < bval_ref[b], xf, 0.0).astype(BF16)
        hg = _dot(xb, wgb[...])
        hu = _dot(xb, wub[...])
        hid = (hg / (1.0 + jnp.exp(-hg)) * hu).astype(BF16)
        y = _dot(hid, wdb[...])
        for c in range(CHUNKS):
            y_ref[pl.ds(c, bm, stride=CHUNKS), :] = y[:, c * LANES:(c + 1) * LANES]


def _moe(bexp, bval, nb_used, xbuf, wg, wu, wd, layer, bm):
    nblk = bexp.shape[0]
    wmap = lambda b, bexp, bval, nb: (layer, bexp[b], 0, 0)
    xmap = lambda b, bexp, bval, nb: (jnp.minimum(b, nb[0] - 1), 0)
    return pl.pallas_call(
        _moe_kernel,
        out_shape=jax.ShapeDtypeStruct(xbuf.shape, F32),
        grid_spec=pltpu.PrefetchScalarGridSpec(
            num_scalar_prefetch=3,
            grid=(nblk,),
            in_specs=[
                pl.BlockSpec((bm * CHUNKS, LANES), xmap),
                pl.BlockSpec((1, 1, D_MODEL, D_EXPERT), wmap),
                pl.BlockSpec((1, 1, D_MODEL, D_EXPERT), wmap),
                pl.BlockSpec((1, 1, D_EXPERT, D_MODEL), wmap),
            ],
            out_specs=pl.BlockSpec((bm * CHUNKS, LANES), lambda b, bexp, bval, nb: (b, 0)),
            scratch_shapes=[pltpu.VMEM((D_MODEL, D_EXPERT), BF16),
                            pltpu.VMEM((D_MODEL, D_EXPERT), BF16),
                            pltpu.VMEM((D_EXPERT, D_MODEL), BF16)]),
        compiler_params=pltpu.CompilerParams(
            dimension_semantics=("arbitrary",), vmem_limit_bytes=VMEM_LIMIT),
        name="experts",
    )(bexp, bval, nb_used, xbuf, wg, wu, wd)


def _comb_kernel(cur_ref, nxt_ref, x_ref, wt_ref, fg_ref, y_ref, o_ref, ybuf, sem, *, final):
    tm = x_ref.shape[0]
    i = pl.program_id(0)
    n = pl.num_programs(0)
    slot = i % 2

    @pl.when(i == 0)
    def _():
        _start_row_gather(cur_ref, 2 * tm, y_ref, ybuf, 0, sem)

    @pl.when(i + 1 < n)
    def _():
        _start_row_gather(nxt_ref, 2 * tm, y_ref, ybuf, 1 - slot, sem)

    def compute(s):
        _wait_row_gather(2 * tm, y_ref, ybuf, s, sem)
        y0 = _rows_from_tiles(ybuf, s, 0, tm)
        y1 = _rows_from_tiles(ybuf, s, tm, tm)
        wt = wt_ref[...]
        xo = x_ref[...] + wt[:, 4:5] * y0 + wt[:, 5:6] * y1
        if final:
            xo = _rms(xo, fg_ref[...])
        o_ref[...] = xo

    for s in range(2):
        @pl.when(slot == s)
        def _(s=s):
            compute(s)


def _combine(dest, x2d, wt, fg, ybuf, *, final):
    t = x2d.shape[0]
    tm = dest.shape[2] // TOP_K
    nsteps = t // tm
    row = lambda i: (i, 0)
    return pl.pallas_call(
        functools.partial(_comb_kernel, final=final),
        out_shape=jax.ShapeDtypeStruct((t, D_MODEL), F32),
        grid=(nsteps,),
        in_specs=[
            pl.BlockSpec((1, 1, 2 * tm), lambda i: (i, 0, 0), memory_space=pltpu.SMEM),
            pl.BlockSpec((1, 1, 2 * tm), lambda i: (jnp.minimum(i + 1, nsteps - 1), 0, 0),
                         memory_space=pltpu.SMEM),
            pl.BlockSpec((tm, D_MODEL), row),
            pl.BlockSpec((tm, SUBLANES), row),
            pl.BlockSpec((1, D_MODEL), lambda i: (0, 0)),
            pl.BlockSpec(memory_space=pl.ANY),
        ],
        out_specs=pl.BlockSpec((tm, D_MODEL), row),
        scratch_shapes=[pltpu.VMEM((2, 2 * tm * CHUNKS, LANES), F32),
                        pltpu.SemaphoreType.DMA((2,))],
        compiler_params=pltpu.CompilerParams(
            dimension_semantics=("arbitrary",), vmem_limit_bytes=VMEM_LIMIT),
        name="combine",
    )(dest, dest, x2d, wt, fg, ybuf)


def _rope_tables():
    pos = jnp.arange(SEQ, dtype=F32)
    inv = ROPE_THETA ** (-jnp.arange(0, HEAD_DIM, 2, dtype=F32) / HEAD_DIM)
    ang1 = pos[:, None] * inv[None, :]
    rows = SEQ // GRID_W
    r = jnp.broadcast_to(jnp.arange(rows, dtype=F32)[:, None], (rows, GRID_W)).reshape(-1)
    c = jnp.broadcast_to(jnp.arange(GRID_W, dtype=F32)[None, :], (rows, GRID_W)).reshape(-1)
    axis_dim = HEAD_DIM // 2
    inv2 = ROPE_THETA ** (-jnp.arange(0, axis_dim, 2, dtype=F32) / axis_dim)
    ang2 = jnp.concatenate([r[:, None] * inv2, c[:, None] * inv2], axis=-1)

    def tables(ang):
        cs, sn = jnp.cos(ang), jnp.sin(ang)
        reps = LANES // HEAD_DIM
        return (jnp.tile(jnp.concatenate([cs, cs], axis=1), (1, reps)),
                jnp.tile(jnp.concatenate([-sn, sn], axis=1), (1, reps)))

    return tables(ang1) + tables(ang2)


def kernel(x, mem, mem_norm_g, w_mem_kv, norm1_g, w_in, lam_q1, lam_k1, lam_q2, lam_k2,
           subln_g, q_norm_g, k_norm_g, w_up_a, w_up_b, w_up_c, w_out, norm2_g,
           w_router_group, b_router_group, w_router_expert, b_router_expert,
           w_exp_gate, w_exp_up, w_exp_down, final_norm_g):
    batch, seq, d = x.shape
    depth = w_in.shape[0]
    assert (seq, d, mem.shape[1]) == (SEQ, D_MODEL, MEM_LEN)
    t = batch * seq
    ca, sa, cb, sb = _rope_tables()
    gidx = jnp.arange(512) // HEAD_DIM
    bd = (gidx[:, None] == gidx[None, :]).astype(BF16)
    ti = jnp.arange(TM_POST)
    tri = (ti[:, None] < ti[None, :]).astype(BF16)
    buf_rows = t * TOP_K + N_EXPERTS * BM * 2

    mkv = _memkv(mem.reshape(batch * MEM_LEN, d), mem_norm_g.reshape(1, d), w_mem_kv.astype(BF16))
    x2d = x.reshape(t, d)
    ybuf = None
    for l in range(depth):
        wl = w_in[l]
        bk0, bk1 = wl[:, 2048:2112], wl[:, 2112:2176]
        bv0, bv1 = wl[:, 2176:2240], wl[:, 2240:2304]
        w = jnp.concatenate([wl[:, :2048], bk0, bk0, bk1, bk1, bv0, bv0, bv1, bv1, wl[:, 2304:]],
                            axis=1).astype(BF16)
        aq, ak, av, bq, bk, bv, cq, gate = _in_proj(
            x2d, norm1_g[l].reshape(1, d), w, ca, sa, cb, sb,
            jnp.tile(q_norm_g[l], 8).reshape(1, 512), jnp.tile(k_norm_g[l], 4).reshape(1, 256), bd)

        lam_init = 0.8 - 0.6 * math.exp(-0.3 * l)
        lamp = jnp.stack([lam_q1[l], lam_k1[l], lam_q2[l], lam_k2[l]]).astype(F32)
        oa = _attention(aq, ak, av, batch, diff=True, lamp=lamp,
                        gs=subln_g[l].reshape(LANES, 1), lam_init=lam_init)
        ob = _attention(bq, bk, bv, batch, diff=False)

        wr = jnp.zeros((ROUTE_ROWS, d), F32)
        wr = wr.at[0:N_GROUPS].set(w_router_group[l].T).at[SUBLANES:].set(w_router_expert[l].T)
        wr_hi = wr.astype(BF16)
        wr_lo = (wr - wr_hi.astype(F32)).astype(BF16)
        br = jnp.zeros((ROUTE_ROWS, 1), F32)
        br = br.at[0:N_GROUPS, 0].set(b_router_group[l]).at[SUBLANES:, 0].set(b_router_expert[l])
        x2d, h2, route, cnt = _post(
            x2d, oa, ob, cq, gate, mkv, w_up_a[l].astype(BF16), w_up_b[l].astype(BF16),
            w_up_c[l].astype(BF16), w_out[l].astype(BF16), norm2_g[l].reshape(1, d),
            jnp.concatenate([wr_hi, wr_lo], axis=0), br, tri)

        bm = BM * (l + 1)
        nblk = buf_rows // bm
        counts = cnt[:, 0].astype(jnp.int32)
        nblk_e = (counts + bm - 1) // bm
        cum = jnp.cumsum(nblk_e)
        first = cum - nblk_e
        nb_used = cum[-1].reshape(1)
        bidx = jnp.minimum(jnp.arange(nblk, dtype=jnp.int32), nb_used - 1)
        bexp = jnp.sum((cum[None, :] <= bidx[:, None]).astype(jnp.int32), axis=1)
        bval = jnp.clip(counts[bexp] - (bidx - first[bexp]) * bm, 0, bm)
        experts = route[0:2].astype(jnp.int32)
        hit = experts[:, :, None] == jnp.arange(N_EXPERTS, dtype=jnp.int32)
        slots = (jnp.sum(jnp.where(hit, first * bm, 0), axis=-1)
                 + route[2:4].astype(jnp.int32))

        if ybuf is None:
            ybuf = jnp.zeros((buf_rows * CHUNKS, LANES), F32)
        xbuf = _dispatch(slots.T.reshape(t // TD, 1, TOP_K * TD), h2, ybuf, l == 0)
        ybuf = _moe(bexp, bval, nb_used, xbuf, w_exp_gate, w_exp_up, w_exp_down, l, bm)

        tmc = TM_COMB * (l + 1)
        dest_blk = slots.reshape(2, t // tmc, tmc).transpose(1, 0, 2).reshape(
            t // tmc, 1, 2 * tmc)
        x2d = _combine(dest_blk, x2d, route.T, final_norm_g.reshape(1, d), ybuf,
                       final=(l == depth - 1))
    return x2d.reshape(batch, seq, d)
```

```python
import functools
import math

import jax
import jax.numpy as jnp
from jax import lax
from jax.experimental import pallas as pl
from jax.experimental.pallas import tpu as pltpu

F32 = jnp.float32
BF16 = jnp.bfloat16

D_MODEL = 1024
SEQ = 2048
MEM_LEN = 256
HEAD_DIM = 64
MX_HEAD_DIM = 128
BRANCH_W = 512
GRID_W = 64
ROPE_THETA = 10000.0
NORM_EPS = 1e-6
N_GROUPS = 4
EXPERTS_PER_GROUP = 8
N_EXPERTS = N_GROUPS * EXPERTS_PER_GROUP
TOP_K = 2
D_EXPERT = 512

LANES = 128
SUBLANES = 8
CHUNKS = D_MODEL // LANES

TM_IN = 512
TQ = 1024
TK = 256
ONES_ROWS = 16
LOG2E = math.log2(math.e)
TM_POST = 512
BM = 512
TM_COMB = 256
TD = 1024
ISSUE_UNROLL = 8
VMEM_LIMIT = 56 * 1024 * 1024

C_AQ, C_AK, C_AV, C_BQ, C_BK, C_BV, C_CQ, C_G, C_END = (
    0, 512, 1024, 1536, 2048, 2304, 2560, 3072, 6144)
ROUTE_ROWS = 40


def _rms(xf, g):
    ms = jnp.mean(xf * xf, axis=-1, keepdims=True)
    return xf * lax.rsqrt(ms + NORM_EPS) * g


def _dot(a, b):
    return jnp.dot(a, b, preferred_element_type=F32)


def _dot_nt(a, b):
    return lax.dot_general(a, b, (((1,), (1,)), ((), ())), preferred_element_type=F32)


def _memkv_kernel(m_ref, g_ref, w_ref, o_ref):
    h = _rms(m_ref[...], g_ref[...]).astype(BF16)
    o_ref[...] = _dot(h, w_ref[...]).astype(BF16)


def _memkv(mem2d, g, w):
    n = mem2d.shape[0]
    tm = 512
    return pl.pallas_call(
        _memkv_kernel,
        out_shape=jax.ShapeDtypeStruct((n, w.shape[1]), BF16),
        grid=(n // tm,),
        in_specs=[pl.BlockSpec((tm, D_MODEL), lambda i: (i, 0)),
                  pl.BlockSpec((1, D_MODEL), lambda i: (0, 0)),
                  pl.BlockSpec(w.shape, lambda i: (0, 0))],
        out_specs=pl.BlockSpec((tm, w.shape[1]), lambda i: (i, 0)),
        compiler_params=pltpu.CompilerParams(
            dimension_semantics=("arbitrary",), vmem_limit_bytes=VMEM_LIMIT),
        name="memkv",
    )(mem2d, g, w)


def _in_kernel(x_ref, g1_ref, w_ref, ca_ref, sa_ref, cb_ref, sb_ref, qg_ref, kg_ref, bd_ref,
               aq_ref, ak_ref, av_ref, bq_ref, bk_ref, bv_ref, cq_ref, gate_ref):
    tm = x_ref.shape[0]
    h = _rms(x_ref[...], g1_ref[...]).astype(BF16)
    lane = lax.broadcasted_iota(jnp.int32, (tm, LANES), 1)
    first_half = (lane & (HEAD_DIM // 2)) == 0

    def seg(lo, hi):
        return _dot(h, w_ref[:, lo:hi])

    def rope(p, c, s):
        sw = jnp.where(first_half, pltpu.roll(p, LANES - HEAD_DIM // 2, 1),
                       pltpu.roll(p, HEAD_DIM // 2, 1))
        return p * c + sw * s

    def group_norm(p, gain):
        n = p.shape[1]
        ss = _dot((p * p).astype(BF16), bd_ref[:n, :n])
        return p * lax.rsqrt(ss * (1.0 / HEAD_DIM) + NORM_EPS) * gain

    def rope_store(p, c_ref, s_ref, o_ref, scale):
        c = c_ref[...]
        s = s_ref[...]
        for j in range(p.shape[1] // LANES):
            sl = slice(j * LANES, (j + 1) * LANES)
            o_ref[:, sl] = (rope(p[:, sl], c, s) * scale).astype(BF16)

    q_scale = HEAD_DIM ** -0.5 * LOG2E
    rope_store(seg(C_AQ, C_AK), ca_ref, sa_ref, aq_ref, q_scale)
    rope_store(seg(C_AK, C_AV), ca_ref, sa_ref, ak_ref, 1.0)
    av_ref[...] = seg(C_AV, C_BQ).T.astype(BF16)
    rope_store(group_norm(seg(C_BQ, C_BK), qg_ref[...]), cb_ref, sb_ref, bq_ref, q_scale)
    rope_store(group_norm(seg(C_BK, C_BV), kg_ref[...]), cb_ref, sb_ref, bk_ref, 1.0)
    bv_ref[...] = seg(C_BV, C_CQ).T.astype(BF16)
    cq_ref[...] = (seg(C_CQ, C_G) * (MX_HEAD_DIM ** -0.5)).astype(BF16)
    for j in range((C_END - C_G) // 512):
        lo = C_G + j * 512
        z = seg(lo, lo + 512)
        gate_ref[:, j * 512:(j + 1) * 512] = (1.0 / (1.0 + jnp.exp(-z))).astype(BF16)


def _in_proj(x2d, g1, w, ca, sa, cb, sb, qg, kg, bd):
    t = x2d.shape[0]
    tm = TM_IN
    nrb = SEQ // tm
    row = lambda i: (i, 0)
    const = lambda i: (0, 0)
    tab = lambda i: (i % nrb, 0)
    outs = ((512, False), (512, False), (512, True), (512, False), (256, False), (256, True),
            (512, False), (3072, False))
    col = lambda i: (0, i)
    return pl.pallas_call(
        _in_kernel,
        out_shape=[jax.ShapeDtypeStruct((n, t) if tr else (t, n), BF16) for n, tr in outs],
        grid=(t // tm,),
        in_specs=[pl.BlockSpec((tm, D_MODEL), row),
                  pl.BlockSpec((1, D_MODEL), const),
                  pl.BlockSpec(w.shape, const),
                  pl.BlockSpec((tm, LANES), tab), pl.BlockSpec((tm, LANES), tab),
                  pl.BlockSpec((tm, LANES), tab), pl.BlockSpec((tm, LANES), tab),
                  pl.BlockSpec((1, 512), const), pl.BlockSpec((1, 256), const),
                  pl.BlockSpec((512, 512), const)],
        out_specs=[pl.BlockSpec((n, tm), col) if tr else pl.BlockSpec((tm, n), row)
                   for n, tr in outs],
        compiler_params=pltpu.CompilerParams(
            dimension_semantics=("arbitrary",), vmem_limit_bytes=VMEM_LIMIT),
        name="in_proj",
    )(x2d, g1, w, ca, sa, cb, sb, qg, kg, bd)


def _attn_kernel(*refs, diff, post_scale, lam_init, zero_fill=False):
    if diff:
        lamp_ref, gs_ref, q_ref, k_ref, vt_ref, o_ref = refs
    elif zero_fill:
        q_ref, k_ref, vt_ref, o_ref, z_ref = refs
        z_ref[...] = jnp.zeros_like(z_ref)
    else:
        q_ref, k_ref, vt_ref, o_ref = refs
    tq = q_ref.shape[0]
    q = q_ref[...]
    lane = lax.broadcasted_iota(jnp.int32, (tq, LANES), 1)
    lo = lane < HEAD_DIM
    zero = jnp.zeros_like(q)
    qs = jnp.concatenate([jnp.where(lo, q, zero), jnp.where(lo, zero, q)], axis=0)
    cols = 2 * tq
    ones = jnp.ones((ONES_ROWS, TK), BF16)
    m = jnp.full((1, cols), -jnp.inf, F32)
    acc = jnp.zeros((LANES + ONES_ROWS, cols), F32)
    nchunks = SEQ // TK

    def scores(j):
        return _dot_nt(k_ref[j * TK:(j + 1) * TK, :], qs)

    st_next = scores(0)
    for j in range(nchunks):
        st = st_next
        if j + 1 < nchunks:
            st_next = scores(j + 1)
        vtj = jnp.concatenate([vt_ref[:, j * TK:(j + 1) * TK], ones], axis=0)
        m_new = jnp.maximum(m, jnp.max(st, axis=0, keepdims=True))
        alpha = jnp.exp2(m - m_new)
        e = jnp.exp2(st - m_new).astype(BF16)
        acc = alpha * acc + _dot(vtj, e)
        m = m_new
    o = acc[:LANES] / acc[LANES:LANES + 1]
    if diff:
        lp = lamp_ref[...]
        lam = (jnp.exp(jnp.sum(lp[0:1] * lp[1:2], axis=-1, keepdims=True))
               - jnp.exp(jnp.sum(lp[2:3] * lp[3:4], axis=-1, keepdims=True)) + lam_init)
        d = o[:, :tq] - lam * o[:, tq:]
        ms = jnp.mean(d * d, axis=0, keepdims=True)
        out_t = d * lax.rsqrt(ms + NORM_EPS) * gs_ref[...] * post_scale
    else:
        row = lax.broadcasted_iota(jnp.int32, (LANES, tq), 0)
        out_t = jnp.where(row < HEAD_DIM, o[:, :tq], o[:, tq:])
    o_ref[...] = out_t.T.astype(BF16)


def _attention(q, k, vt, batch, *, diff, lamp=None, gs=None, lam_init=0.0, zero_rows=0):
    t = q.shape[0]
    nq = SEQ // TQ
    nblk = q.shape[1] // LANES
    kv_per = nblk // (k.shape[1] // LANES)
    qmap = lambda b, h, i: (b * nq + i, h)
    out_shape = jax.ShapeDtypeStruct((t, q.shape[1]), BF16)
    out_specs = pl.BlockSpec((TQ, LANES), qmap)
    if zero_rows:
        zblk = zero_rows // (batch * nblk * nq)
        out_shape = [out_shape, jax.ShapeDtypeStruct((zero_rows, LANES), F32)]
        out_specs = [out_specs,
                     pl.BlockSpec((zblk, LANES), lambda b, h, i: ((b * nblk + h) * nq + i, 0))]
    in_specs = [pl.BlockSpec((TQ, LANES), qmap),
                pl.BlockSpec((SEQ, LANES), lambda b, h, i: (b, h // kv_per)),
                pl.BlockSpec((LANES, SEQ), lambda b, h, i: (h // kv_per, b))]
    args = [q, k, vt]
    if diff:
        const = lambda b, h, i: (0, 0)
        in_specs = [pl.BlockSpec((4, HEAD_DIM), const), pl.BlockSpec((LANES, 1), const)] + in_specs
        args = [lamp, gs] + args
    return pl.pallas_call(
        functools.partial(_attn_kernel, diff=diff, post_scale=1.0 - lam_init, lam_init=lam_init,
                          zero_fill=bool(zero_rows)),
        out_shape=out_shape,
        grid=(batch, nblk, nq),
        in_specs=in_specs,
        out_specs=out_specs,
        compiler_params=pltpu.CompilerParams(
            dimension_semantics=("arbitrary", "arbitrary", "arbitrary"),
            vmem_limit_bytes=VMEM_LIMIT),
        name="attn_diff" if diff else "attn_gqa",
    )(*args)


def _post_kernel(x_ref, oa_ref, ob_ref, cq_ref, gate_ref, mkv_ref, wa_ref, wb_ref, wc_ref,
                 wo_ref, g2_ref, wr_ref, br_ref, tri_ref,
                 xo_ref, h2_ref, route_ref, wt_ref, cnt_ref, carry_ref):
    tm = x_ref.shape[0]
    i = pl.program_id(0)

    @pl.when(i == 0)
    def _():
        carry_ref[...] = jnp.zeros_like(carry_ref)

    heads = []
    for hd in range(BRANCH_W // MX_HEAD_DIM):
        sl = slice(hd * MX_HEAD_DIM, (hd + 1) * MX_HEAD_DIM)
        sv = slice(BRANCH_W + hd * MX_HEAD_DIM, BRANCH_W + (hd + 1) * MX_HEAD_DIM)
        s = _dot_nt(cq_ref[:, sl], mkv_ref[:, sl])
        e = jnp.exp(s - jnp.max(s, axis=-1, keepdims=True))
        den = jnp.sum(e, axis=-1, keepdims=True)
        heads.append((_dot(e.astype(BF16), mkv_ref[:, sv]) / den).astype(BF16))
    oc = jnp.concatenate(heads, axis=1)

    ya = _dot(oa_ref[...], wa_ref[...])
    yb = _dot(ob_ref[...], wb_ref[...])
    yc = _dot(oc, wc_ref[...])
    merged = (gate_ref[:, 0:D_MODEL].astype(F32) * ya
              + gate_ref[:, D_MODEL:2 * D_MODEL].astype(F32) * yb
              + gate_ref[:, 2 * D_MODEL:3 * D_MODEL].astype(F32) * yc)
    xn = x_ref[...] + _dot(merged.astype(BF16), wo_ref[...])
    xo_ref[...] = xn
    h2 = _rms(xn, g2_ref[...])
    for c in range(CHUNKS):
        h2_ref[pl.ds(c, tm, stride=CHUNKS), :] = h2[:, c * LANES:(c + 1) * LANES]

    h_hi = h2.astype(BF16)
    h_lo = (h2 - h_hi.astype(F32)).astype(BF16)
    l2 = _dot_nt(wr_ref[...], h_hi)
    logits = (l2[:ROUTE_ROWS] + l2[ROUTE_ROWS:] + _dot_nt(wr_ref[:ROUTE_ROWS, :], h_lo)
              + br_ref[...])

    neg = -jnp.inf
    r8 = lax.broadcasted_iota(jnp.int32, (SUBLANES, tm), 0)
    r32 = lax.broadcasted_iota(jnp.int32, (N_EXPERTS, tm), 0)
    gl = jnp.where(r8 < N_GROUPS, logits[0:SUBLANES], neg)
    gmax = jnp.max(gl, axis=0, keepdims=True)
    gidx = jnp.min(jnp.where(gl == gmax, r8, SUBLANES), axis=0, keepdims=True)
    gp = 1.0 / jnp.sum(jnp.exp(gl - gmax), axis=0, keepdims=True)
    el = jnp.where((r32 // EXPERTS_PER_GROUP) == gidx, logits[SUBLANES:ROUTE_ROWS], neg)
    m1 = jnp.max(el, axis=0, keepdims=True)
    i1 = jnp.min(jnp.where(el == m1, r32, N_EXPERTS), axis=0, keepdims=True)
    el2 = jnp.where(r32 == i1, neg, el)
    m2 = jnp.max(el2, axis=0, keepdims=True)
    i2 = jnp.min(jnp.where(el2 == m2, r32, N_EXPERTS), axis=0, keepdims=True)
    d = jnp.exp(m2 - m1)
    w1 = gp / (1.0 + d)
    w2 = gp * d / (1.0 + d)

    hit1 = r32 == i1
    hit2 = r32 == i2
    oh = jnp.where(hit1 | hit2, 1.0, 0.0)
    before = _dot(oh.astype(BF16), tri_ref[...]) + carry_ref[...]
    rank1 = jnp.sum(jnp.where(hit1, before, 0.0), axis=0, keepdims=True)
    rank2 = jnp.sum(jnp.where(hit2, before, 0.0), axis=0, keepdims=True)
    carry_ref[...] = carry_ref[...] + jnp.sum(oh, axis=1, keepdims=True)
    zrow = jnp.zeros_like(w1)
    route = jnp.concatenate(
        [i1.astype(F32), i2.astype(F32), rank1, rank2, w1, w2, zrow, zrow], axis=0)
    route_ref[...] = route
    wt_ref[...] = jnp.concatenate(
        [route, jnp.zeros((LANES - SUBLANES, tm), F32)], axis=0).T
    cnt_ref[...] = jnp.broadcast_to(carry_ref[...], cnt_ref.shape)


def _post(x2d, oa, ob, cq, gate, mkv, wa, wb, wc, wo, g2, wr, br, tri):
    t = x2d.shape[0]
    tm = TM_POST
    nrb = SEQ // tm
    row = lambda i: (i, 0)
    const = lambda i: (0, 0)
    return pl.pallas_call(
        _post_kernel,
        out_shape=[jax.ShapeDtypeStruct((t, D_MODEL), F32),
                   jax.ShapeDtypeStruct((t * CHUNKS, LANES), F32),
                   jax.ShapeDtypeStruct((SUBLANES, t), F32),
                   jax.ShapeDtypeStruct((t, LANES), F32),
                   jax.ShapeDtypeStruct((N_EXPERTS, LANES), F32)],
        grid=(t // tm,),
        in_specs=[pl.BlockSpec((tm, D_MODEL), row),
                  pl.BlockSpec((tm, BRANCH_W), row),
                  pl.BlockSpec((tm, BRANCH_W), row),
                  pl.BlockSpec((tm, BRANCH_W), row),
                  pl.BlockSpec((tm, 3 * D_MODEL), row),
                  pl.BlockSpec((MEM_LEN, 2 * BRANCH_W), lambda i: (i // nrb, 0)),
                  pl.BlockSpec(wa.shape, const), pl.BlockSpec(wb.shape, const),
                  pl.BlockSpec(wc.shape, const), pl.BlockSpec(wo.shape, const),
                  pl.BlockSpec((1, D_MODEL), const),
                  pl.BlockSpec(wr.shape, const), pl.BlockSpec(br.shape, const),
                  pl.BlockSpec(tri.shape, const)],
        out_specs=[pl.BlockSpec((tm, D_MODEL), row),
                   pl.BlockSpec((tm * CHUNKS, LANES), row),
                   pl.BlockSpec((SUBLANES, tm), lambda i: (0, i)),
                   pl.BlockSpec((tm, LANES), row),
                   pl.BlockSpec((N_EXPERTS, LANES), const)],
        scratch_shapes=[pltpu.VMEM((N_EXPERTS, 1), F32)],
        compiler_params=pltpu.CompilerParams(
            dimension_semantics=("arbitrary",), vmem_limit_bytes=VMEM_LIMIT),
        name="post",
    )(x2d, oa, ob, cq, gate, mkv, wa, wb, wc, wo, g2, wr, br, tri)


def _plan_kernel(cnt_ref, route_ref, ltri_ref, slots_ref, plan_ref, *, bm):
    t = route_ref.shape[1]
    counts = cnt_ref[...]
    nblk_e = jnp.floor((counts + (bm - 1)) * (1.0 / bm))
    cum = _dot(ltri_ref[...], nblk_e.astype(BF16))
    first = cum - nblk_e
    first_row = first[:, 0:1] * bm
    e_tok = lax.broadcasted_iota(jnp.int32, (N_EXPERTS, t), 0)
    rows = []
    for k in range(TOP_K):
        e_k = route_ref[k:k + 1, :].astype(jnp.int32)
        base = jnp.sum(jnp.where(e_tok == e_k, first_row, 0.0), axis=0, keepdims=True)
        rows.append((base + route_ref[TOP_K + k:TOP_K + k + 1, :]).astype(jnp.int32))
    slots_ref[...] = jnp.concatenate(
        rows + [jnp.zeros((SUBLANES - TOP_K, t), jnp.int32)], axis=0)

    nb_used = cum[N_EXPERTS - 1:N_EXPERTS, :]
    b = lax.broadcasted_iota(jnp.int32, (1, LANES), 1).astype(F32)
    bidx = jnp.minimum(b, nb_used - 1.0)
    bexp = jnp.sum(jnp.where(cum <= bidx, 1.0, 0.0), axis=0, keepdims=True)
    sel = lax.broadcasted_iota(jnp.int32, (N_EXPERTS, LANES), 0).astype(F32) == bexp
    cnt_at = jnp.sum(jnp.where(sel, counts, 0.0), axis=0, keepdims=True)
    first_at = jnp.sum(jnp.where(sel, first, 0.0), axis=0, keepdims=True)
    bval = jnp.clip(cnt_at - (bidx - first_at) * bm, 0.0, float(bm))
    plan_ref[...] = jnp.concatenate(
        [bexp, bval, nb_used, jnp.zeros((SUBLANES - 3, LANES), F32)], axis=0).astype(jnp.int32)


def _plan(cnt, route, ltri, bm):
    t = route.shape[1]
    full = lambda shape: pl.BlockSpec(shape, lambda i: (0, 0))
    return pl.pallas_call(
        functools.partial(_plan_kernel, bm=bm),
        out_shape=[jax.ShapeDtypeStruct((SUBLANES, t), jnp.int32),
                   jax.ShapeDtypeStruct((SUBLANES, LANES), jnp.int32)],
        grid=(1,),
        in_specs=[full(cnt.shape), full(route.shape), full(ltri.shape)],
        out_specs=[full((SUBLANES, t)), full((SUBLANES, LANES))],
        compiler_params=pltpu.CompilerParams(
            dimension_semantics=("arbitrary",), vmem_limit_bytes=VMEM_LIMIT),
        name="plan",
    )(cnt, route, ltri)


def _start_row_gather(idx_ref, n, src_ref, dst_ref, slot, sem):
    def body(i, carry):
        for u in range(ISSUE_UNROLL // TOP_K):
            r = i * (ISSUE_UNROLL // TOP_K) + u
            for k in range(TOP_K):
                row = idx_ref[k, r]
                pltpu.make_async_copy(
                    src_ref.at[pl.ds(pl.multiple_of(row * CHUNKS, CHUNKS), CHUNKS)],
                    dst_ref.at[slot, pl.ds(pl.multiple_of((k * n + r) * CHUNKS, CHUNKS), CHUNKS)],
                    sem.at[slot]).start(priority=k)
        return carry
    lax.fori_loop(0, n // (ISSUE_UNROLL // TOP_K), body, 0)


def _wait_row_gather(n, src_ref, dst_ref, slot, sem):
    pltpu.make_async_copy(src_ref.at[pl.ds(0, n * CHUNKS)], dst_ref.at[slot], sem.at[slot]).wait()


def _rows_from_tiles(buf_ref, slot, first, n):
    return jnp.concatenate(
        [buf_ref[slot, pl.ds(first * CHUNKS + c, n, stride=CHUNKS), :] for c in range(CHUNKS)],
        axis=1)


def _dispatch_kernel(idx_ref, h2_ref, xin_ref, xbuf_ref, sem):
    del xin_ref
    td = idx_ref.shape[1]

    def body(i, carry):
        for u in range(ISSUE_UNROLL // TOP_K):
            tok = i * (ISSUE_UNROLL // TOP_K) + u
            for k in range(TOP_K):
                slot_row = idx_ref[k, tok]
                pltpu.make_async_copy(
                    h2_ref.at[pl.ds(pl.multiple_of(tok * CHUNKS, CHUNKS), CHUNKS)],
                    xbuf_ref.at[pl.ds(pl.multiple_of(slot_row * CHUNKS, CHUNKS), CHUNKS)],
                    sem.at[0]).start(priority=k)
        return carry
    lax.fori_loop(0, td // (ISSUE_UNROLL // TOP_K), body, 0)
    for _ in range(TOP_K):
        pltpu.make_async_copy(h2_ref, xbuf_ref.at[pl.ds(0, h2_ref.shape[0])], sem.at[0]).wait()


def _dispatch(slots, h2, xbuf_init):
    td = TD
    nsteps = slots.shape[1] // td
    return pl.pallas_call(
        _dispatch_kernel,
        out_shape=jax.ShapeDtypeStruct(xbuf_init.shape, F32),
        grid=(nsteps,),
        in_specs=[pl.BlockSpec((SUBLANES, td), lambda i: (0, i), memory_space=pltpu.SMEM),
                  pl.BlockSpec((td * CHUNKS, LANES), lambda i: (i, 0)),
                  pl.BlockSpec(memory_space=pl.ANY)],
        out_specs=pl.BlockSpec(memory_space=pl.ANY),
        input_output_aliases={2: 0},
        scratch_shapes=[pltpu.SemaphoreType.DMA((1,))],
        compiler_params=pltpu.CompilerParams(
            dimension_semantics=("arbitrary",), vmem_limit_bytes=VMEM_LIMIT),
        name="dispatch",
    )(slots, h2, xbuf_init)


def _moe_kernel(bexp_ref, bval_ref, nb_ref, x_ref, wg_ref, wu_ref, wd_ref, y_ref,
                wgb, wub, wdb):
    b = pl.program_id(0)

    @pl.when((b == 0) | (bexp_ref[b] != bexp_ref[jnp.maximum(b - 1, 0)]))
    def _():
        wgb[...] = wg_ref[0, 0].astype(BF16)
        wub[...] = wu_ref[0, 0].astype(BF16)
        wdb[...] = wd_ref[0, 0].astype(BF16)

    @pl.when(b >= nb_ref[0])
    def _():
        y_ref[...] = jnp.zeros_like(y_ref)

    @pl.when(b < nb_ref[0])
    def _():
        bm = x_ref.shape[0] // CHUNKS
        xf = jnp.concatenate(
            [x_ref[pl.ds(c, bm, stride=CHUNKS), :] for c in range(CHUNKS)], axis=1)
        rowid = lax.broadcasted_iota(jnp.int32, (bm, 1), 0)
        xb = jnp.where(rowid < bval_ref[b], xf, 0.0).astype(BF16)
        hg = _dot(xb, wgb[...])
        hu = _dot(xb, wub[...])
        hid = (hg / (1.0 + jnp.exp(-hg)) * hu).astype(BF16)
        y = _dot(hid, wdb[...])
        for c in range(CHUNKS):
            y_ref[pl.ds(c, bm, stride=CHUNKS), :] = y[:, c * LANES:(c + 1) * LANES]


def _moe(bexp, bval, nb_used, xbuf, wg, wu, wd, layer, bm):
    nblk = bexp.shape[0]
    wmap = lambda b, bexp, bval, nb: (layer, bexp[b], 0, 0)
    xmap = lambda b, bexp, bval, nb: (jnp.minimum(b, nb[0] - 1), 0)
    return pl.pallas_call(
        _moe_kernel,
        out_shape=jax.ShapeDtypeStruct(xbuf.shape, F32),
        grid_spec=pltpu.PrefetchScalarGridSpec(
            num_scalar_prefetch=3,
            grid=(nblk,),
            in_specs=[
                pl.BlockSpec((bm * CHUNKS, LANES), xmap),
                pl.BlockSpec((1, 1, D_MODEL, D_EXPERT), wmap),
                pl.BlockSpec((1, 1, D_MODEL, D_EXPERT), wmap),
                pl.BlockSpec((1, 1, D_EXPERT, D_MODEL), wmap),
            ],
            out_specs=pl.BlockSpec((bm * CHUNKS, LANES), lambda b, bexp, bval, nb: (b, 0)),
            scratch_shapes=[pltpu.VMEM((D_MODEL, D_EXPERT), BF16),
                            pltpu.VMEM((D_MODEL, D_EXPERT), BF16),
                            pltpu.VMEM((D_EXPERT, D_MODEL), BF16)]),
        compiler_params=pltpu.CompilerParams(
            dimension_semantics=("arbitrary",), vmem_limit_bytes=VMEM_LIMIT),
        name="experts",
    )(bexp, bval, nb_used, xbuf, wg, wu, wd)


def _comb_kernel(cur_ref, nxt_ref, x_ref, wt_ref, fg_ref, y_ref, o_ref, ybuf, sem, *, final):
    tm = x_ref.shape[0]
    i = pl.program_id(0)
    n = pl.num_programs(0)
    slot = i % 2

    @pl.when(i == 0)
    def _():
        _start_row_gather(cur_ref, tm, y_ref, ybuf, 0, sem)

    @pl.when(i + 1 < n)
    def _():
        _start_row_gather(nxt_ref, tm, y_ref, ybuf, 1 - slot, sem)

    def compute(s):
        _wait_row_gather(2 * tm, y_ref, ybuf, s, sem)
        y0 = _rows_from_tiles(ybuf, s, 0, tm)
        y1 = _rows_from_tiles(ybuf, s, tm, tm)
        wt = wt_ref[...]
        xo = x_ref[...] + wt[:, 4:5] * y0 + wt[:, 5:6] * y1
        if final:
            xo = _rms(xo, fg_ref[...])
        o_ref[...] = xo

    for s in range(2):
        @pl.when(slot == s)
        def _(s=s):
            compute(s)


def _combine(dest, x2d, wt, fg, ybuf, *, final):
    t = x2d.shape[0]
    tm = TM_COMB
    nsteps = t // tm
    row = lambda i: (i, 0)
    return pl.pallas_call(
        functools.partial(_comb_kernel, final=final),
        out_shape=jax.ShapeDtypeStruct((t, D_MODEL), F32),
        grid=(nsteps,),
        in_specs=[
            pl.BlockSpec((SUBLANES, tm), lambda i: (0, i), memory_space=pltpu.SMEM),
            pl.BlockSpec((SUBLANES, tm), lambda i: (0, jnp.minimum(i + 1, nsteps - 1)),
                         memory_space=pltpu.SMEM),
            pl.BlockSpec((tm, D_MODEL), row),
            pl.BlockSpec((tm, LANES), row),
            pl.BlockSpec((1, D_MODEL), lambda i: (0, 0)),
            pl.BlockSpec(memory_space=pl.ANY),
        ],
        out_specs=pl.BlockSpec((tm, D_MODEL), row),
        scratch_shapes=[pltpu.VMEM((2, 2 * tm * CHUNKS, LANES), F32),
                        pltpu.SemaphoreType.DMA((2,))],
        compiler_params=pltpu.CompilerParams(
            dimension_semantics=("arbitrary",), vmem_limit_bytes=VMEM_LIMIT),
        name="combine",
    )(dest, dest, x2d, wt, fg, ybuf)


def _rope_tables():
    pos = jnp.arange(SEQ, dtype=F32)
    inv = ROPE_THETA ** (-jnp.arange(0, HEAD_DIM, 2, dtype=F32) / HEAD_DIM)
    ang1 = pos[:, None] * inv[None, :]
    rows = SEQ // GRID_W
    r = jnp.broadcast_to(jnp.arange(rows, dtype=F32)[:, None], (rows, GRID_W)).reshape(-1)
    c = jnp.broadcast_to(jnp.arange(GRID_W, dtype=F32)[None, :], (rows, GRID_W)).reshape(-1)
    axis_dim = HEAD_DIM // 2
    inv2 = ROPE_THETA ** (-jnp.arange(0, axis_dim, 2, dtype=F32) / axis_dim)
    ang2 = jnp.concatenate([r[:, None] * inv2, c[:, None] * inv2], axis=-1)

    def tables(ang):
        cs, sn = jnp.cos(ang), jnp.sin(ang)
        reps = LANES // HEAD_DIM
        return (jnp.tile(jnp.concatenate([cs, cs], axis=1), (1, reps)),
                jnp.tile(jnp.concatenate([-sn, sn], axis=1), (1, reps)))

    return tables(ang1) + tables(ang2)


def kernel(x, mem, mem_norm_g, w_mem_kv, norm1_g, w_in, lam_q1, lam_k1, lam_q2, lam_k2,
           subln_g, q_norm_g, k_norm_g, w_up_a, w_up_b, w_up_c, w_out, norm2_g,
           w_router_group, b_router_group, w_router_expert, b_router_expert,
           w_exp_gate, w_exp_up, w_exp_down, final_norm_g):
    batch, seq, d = x.shape
    depth = w_in.shape[0]
    assert (seq, d, mem.shape[1]) == (SEQ, D_MODEL, MEM_LEN)
    t = batch * seq
    ca, sa, cb, sb = _rope_tables()
    gidx = jnp.arange(512) // HEAD_DIM
    bd = (gidx[:, None] == gidx[None, :]).astype(BF16)
    ti = jnp.arange(TM_POST)
    tri = (ti[:, None] < ti[None, :]).astype(BF16)
    ei = jnp.arange(N_EXPERTS)
    ltri = (ei[None, :] <= ei[:, None]).astype(BF16)
    buf_rows = t * TOP_K + N_EXPERTS * BM
    assert buf_rows // BM <= LANES

    mkv = _memkv(mem.reshape(batch * MEM_LEN, d), mem_norm_g.reshape(1, d), w_mem_kv.astype(BF16))
    x2d = x.reshape(t, d)
    ybuf = None
    for l in range(depth):
        wl = w_in[l]
        bk0, bk1 = wl[:, 2048:2112], wl[:, 2112:2176]
        bv0, bv1 = wl[:, 2176:2240], wl[:, 2240:2304]
        w = jnp.concatenate([wl[:, :2048], bk0, bk0, bk1, bk1, bv0, bv0, bv1, bv1, wl[:, 2304:]],
                            axis=1).astype(BF16)
        aq, ak, av, bq, bk, bv, cq, gate = _in_proj(
            x2d, norm1_g[l].reshape(1, d), w, ca, sa, cb, sb,
            jnp.tile(q_norm_g[l], 8).reshape(1, 512), jnp.tile(k_norm_g[l], 4).reshape(1, 256), bd)

        lam_init = 0.8 - 0.6 * math.exp(-0.3 * l)
        lamp = jnp.stack([lam_q1[l], lam_k1[l], lam_q2[l], lam_k2[l]]).astype(F32)
        oa = _attention(aq, ak, av, batch, diff=True, lamp=lamp,
                        gs=subln_g[l].reshape(LANES, 1), lam_init=lam_init)
        if ybuf is None:
            ob, ybuf = _attention(bq, bk, bv, batch, diff=False, zero_rows=buf_rows * CHUNKS)
        else:
            ob = _attention(bq, bk, bv, batch, diff=False)

        wr = jnp.zeros((ROUTE_ROWS, d), F32)
        wr = wr.at[0:N_GROUPS].set(w_router_group[l].T).at[SUBLANES:].set(w_router_expert[l].T)
        wr_hi = wr.astype(BF16)
        wr_lo = (wr - wr_hi.astype(F32)).astype(BF16)
        br = jnp.zeros((ROUTE_ROWS, 1), F32)
        br = br.at[0:N_GROUPS, 0].set(b_router_group[l]).at[SUBLANES:, 0].set(b_router_expert[l])
        x2d, h2, route, wt, cnt = _post(
            x2d, oa, ob, cq, gate, mkv, w_up_a[l].astype(BF16), w_up_b[l].astype(BF16),
            w_up_c[l].astype(BF16), w_out[l].astype(BF16), norm2_g[l].reshape(1, d),
            jnp.concatenate([wr_hi, wr_lo], axis=0), br, tri)

        slots, plan = _plan(cnt, route, ltri, BM)
        nblk = buf_rows // BM
        xbuf = _dispatch(slots, h2, ybuf)
        ybuf = _moe(plan[0, :nblk], plan[1, :nblk], plan[2, :1], xbuf,
                    w_exp_gate, w_exp_up, w_exp_down, l, BM)
        x2d = _combine(slots, x2d, wt, final_norm_g.reshape(1, d), ybuf,
                       final=(l == depth - 1))
    return x2d.reshape(batch, seq, d)
```

```python
import functools
import math

import jax
import jax.numpy as jnp
from jax import lax
from jax.experimental import pallas as pl
from jax.experimental.pallas import tpu as pltpu

F32 = jnp.float32
BF16 = jnp.bfloat16

D_MODEL = 1024
SEQ = 2048
MEM_LEN = 256
HEAD_DIM = 64
MX_HEAD_DIM = 128
BRANCH_W = 512
GRID_W = 64
ROPE_THETA = 10000.0
NORM_EPS = 1e-6
N_GROUPS = 4
EXPERTS_PER_GROUP = 8
N_EXPERTS = N_GROUPS * EXPERTS_PER_GROUP
TOP_K = 2
D_EXPERT = 512

LANES = 128
SUBLANES = 8
CHUNKS = D_MODEL // LANES

TM_IN = 512
TQ = 1024
TK = 256
ONES_ROWS = 16
LOG2E = math.log2(math.e)
TM_POST = 512
BM = 512
TM_COMB = 256
TD = 1024
ISSUE_UNROLL = 16
IDX_BLOCK = 1024
VMEM_LIMIT = 56 * 1024 * 1024

C_AQ, C_AK, C_AV, C_BQ, C_BK, C_BV, C_CQ, C_G, C_END = (
    0, 512, 1024, 1536, 2048, 2304, 2560, 3072, 6144)
ROUTE_ROWS = 40


def _rms(xf, g):
    ms = jnp.mean(xf * xf, axis=-1, keepdims=True)
    return xf * lax.rsqrt(ms + NORM_EPS) * g


def _dot(a, b):
    return jnp.dot(a, b, preferred_element_type=F32)


def _dot_nt(a, b):
    return lax.dot_general(a, b, (((1,), (1,)), ((), ())), preferred_element_type=F32)


def _memkv_kernel(m_ref, g_ref, w_ref, o_ref):
    h = _rms(m_ref[...], g_ref[...]).astype(BF16)
    o_ref[...] = _dot(h, w_ref[...]).astype(BF16)


def _memkv(mem2d, g, w):
    n = mem2d.shape[0]
    tm = 512
    return pl.pallas_call(
        _memkv_kernel,
        out_shape=jax.ShapeDtypeStruct((n, w.shape[1]), BF16),
        grid=(n // tm,),
        in_specs=[pl.BlockSpec((tm, D_MODEL), lambda i: (i, 0)),
                  pl.BlockSpec((1, D_MODEL), lambda i: (0, 0)),
                  pl.BlockSpec(w.shape, lambda i: (0, 0))],
        out_specs=pl.BlockSpec((tm, w.shape[1]), lambda i: (i, 0)),
        compiler_params=pltpu.CompilerParams(
            dimension_semantics=("arbitrary",), vmem_limit_bytes=VMEM_LIMIT),
        name="memkv",
    )(mem2d, g, w)


def _in_kernel(x_ref, g1_ref, w_ref, ca_ref, sa_ref, cb_ref, sb_ref, qg_ref, kg_ref, bd_ref,
               aq_ref, ak_ref, av_ref, bq_ref, bk_ref, bv_ref, cq_ref, gate_ref):
    tm = x_ref.shape[0]
    h = _rms(x_ref[...], g1_ref[...]).astype(BF16)
    lane = lax.broadcasted_iota(jnp.int32, (tm, LANES), 1)
    first_half = (lane & (HEAD_DIM // 2)) == 0

    def seg(lo, hi):
        return _dot(h, w_ref[:, lo:hi])

    def rope(p, c, s):
        sw = jnp.where(first_half, pltpu.roll(p, LANES - HEAD_DIM // 2, 1),
                       pltpu.roll(p, HEAD_DIM // 2, 1))
        return p * c + sw * s

    def group_norm(p, gain):
        n = p.shape[1]
        ss = _dot((p * p).astype(BF16), bd_ref[:n, :n])
        return p * lax.rsqrt(ss * (1.0 / HEAD_DIM) + NORM_EPS) * gain

    def rope_store(p, c_ref, s_ref, o_ref, scale):
        c = c_ref[...]
        s = s_ref[...]
        for j in range(p.shape[1] // LANES):
            sl = slice(j * LANES, (j + 1) * LANES)
            o_ref[:, sl] = (rope(p[:, sl], c, s) * scale).astype(BF16)

    q_scale = HEAD_DIM ** -0.5 * LOG2E
    rope_store(seg(C_AQ, C_AK), ca_ref, sa_ref, aq_ref, q_scale)
    rope_store(seg(C_AK, C_AV), ca_ref, sa_ref, ak_ref, 1.0)
    av_ref[...] = seg(C_AV, C_BQ).T.astype(BF16)
    rope_store(group_norm(seg(C_BQ, C_BK), qg_ref[...]), cb_ref, sb_ref, bq_ref, q_scale)
    rope_store(group_norm(seg(C_BK, C_BV), kg_ref[...]), cb_ref, sb_ref, bk_ref, 1.0)
    bv_ref[...] = seg(C_BV, C_CQ).T.astype(BF16)
    cq_ref[...] = (seg(C_CQ, C_G) * (MX_HEAD_DIM ** -0.5)).astype(BF16)
    for j in range((C_END - C_G) // 512):
        lo = C_G + j * 512
        z = seg(lo, lo + 512)
        gate_ref[:, j * 512:(j + 1) * 512] = (1.0 / (1.0 + jnp.exp(-z))).astype(BF16)


def _in_proj(x2d, g1, w, ca, sa, cb, sb, qg, kg, bd):
    t = x2d.shape[0]
    tm = TM_IN
    nrb = SEQ // tm
    row = lambda i: (i, 0)
    const = lambda i: (0, 0)
    tab = lambda i: (i % nrb, 0)
    outs = ((512, False), (512, False), (512, True), (512, False), (256, False), (256, True),
            (512, False), (3072, False))
    col = lambda i: (0, i)
    return pl.pallas_call(
        _in_kernel,
        out_shape=[jax.ShapeDtypeStruct((n, t) if tr else (t, n), BF16) for n, tr in outs],
        grid=(t // tm,),
        in_specs=[pl.BlockSpec((tm, D_MODEL), row),
                  pl.BlockSpec((1, D_MODEL), const),
                  pl.BlockSpec(w.shape, const),
                  pl.BlockSpec((tm, LANES), tab), pl.BlockSpec((tm, LANES), tab),
                  pl.BlockSpec((tm, LANES), tab), pl.BlockSpec((tm, LANES), tab),
                  pl.BlockSpec((1, 512), const), pl.BlockSpec((1, 256), const),
                  pl.BlockSpec((512, 512), const)],
        out_specs=[pl.BlockSpec((n, tm), col) if tr else pl.BlockSpec((tm, n), row)
                   for n, tr in outs],
        compiler_params=pltpu.CompilerParams(
            dimension_semantics=("arbitrary",), vmem_limit_bytes=VMEM_LIMIT),
        name="in_proj",
    )(x2d, g1, w, ca, sa, cb, sb, qg, kg, bd)


def _attn_kernel(*refs, diff, post_scale, lam_init, zero_fill=False):
    if diff:
        lamp_ref, gs_ref, q_ref, k_ref, vt_ref, o_ref = refs
    elif zero_fill:
        q_ref, k_ref, vt_ref, o_ref, z_ref = refs
        z_ref[...] = jnp.zeros_like(z_ref)
    else:
        q_ref, k_ref, vt_ref, o_ref = refs
    tq = q_ref.shape[0]
    q = q_ref[...]
    lane = lax.broadcasted_iota(jnp.int32, (tq, LANES), 1)
    lo = lane < HEAD_DIM
    zero = jnp.zeros_like(q)
    qs = jnp.concatenate([jnp.where(lo, q, zero), jnp.where(lo, zero, q)], axis=0)
    cols = 2 * tq
    ones = jnp.ones((ONES_ROWS, TK), BF16)
    m = jnp.full((1, cols), -jnp.inf, F32)
    acc = jnp.zeros((LANES + ONES_ROWS, cols), F32)
    nchunks = SEQ // TK

    def scores(j):
        return _dot_nt(k_ref[j * TK:(j + 1) * TK, :], qs)

    st_next = scores(0)
    for j in range(nchunks):
        st = st_next
        if j + 1 < nchunks:
            st_next = scores(j + 1)
        vtj = jnp.concatenate([vt_ref[:, j * TK:(j + 1) * TK], ones], axis=0)
        m_new = jnp.maximum(m, jnp.max(st, axis=0, keepdims=True))
        alpha = jnp.exp2(m - m_new)
        e = jnp.exp2(st - m_new).astype(BF16)
        acc = alpha * acc + _dot(vtj, e)
        m = m_new
    o = acc[:LANES] / acc[LANES:LANES + 1]
    if diff:
        lp = lamp_ref[...]
        lam = (jnp.exp(jnp.sum(lp[0:1] * lp[1:2], axis=-1, keepdims=True))
               - jnp.exp(jnp.sum(lp[2:3] * lp[3:4], axis=-1, keepdims=True)) + lam_init)
        d = o[:, :tq] - lam * o[:, tq:]
        ms = jnp.mean(d * d, axis=0, keepdims=True)
        out_t = d * lax.rsqrt(ms + NORM_EPS) * gs_ref[...] * post_scale
    else:
        row = lax.broadcasted_iota(jnp.int32, (LANES, tq), 0)
        out_t = jnp.where(row < HEAD_DIM, o[:, :tq], o[:, tq:])
    o_ref[...] = out_t.T.astype(BF16)


def _attention(q, k, vt, batch, *, diff, lamp=None, gs=None, lam_init=0.0, zero_rows=0):
    t = q.shape[0]
    nq = SEQ // TQ
    nblk = q.shape[1] // LANES
    kv_per = nblk // (k.shape[1] // LANES)
    qmap = lambda b, h, i: (b * nq + i, h)
    out_shape = jax.ShapeDtypeStruct((t, q.shape[1]), BF16)
    out_specs = pl.BlockSpec((TQ, LANES), qmap)
    if zero_rows:
        zblk = zero_rows // (batch * nblk * nq)
        out_shape = [out_shape, jax.ShapeDtypeStruct((zero_rows, LANES), F32)]
        out_specs = [out_specs,
                     pl.BlockSpec((zblk, LANES), lambda b, h, i: ((b * nblk + h) * nq + i, 0))]
    in_specs = [pl.BlockSpec((TQ, LANES), qmap),
                pl.BlockSpec((SEQ, LANES), lambda b, h, i: (b, h // kv_per)),
                pl.BlockSpec((LANES, SEQ), lambda b, h, i: (h // kv_per, b))]
    args = [q, k, vt]
    if diff:
        const = lambda b, h, i: (0, 0)
        in_specs = [pl.BlockSpec((4, HEAD_DIM), const), pl.BlockSpec((LANES, 1), const)] + in_specs
        args = [lamp, gs] + args
    return pl.pallas_call(
        functools.partial(_attn_kernel, diff=diff, post_scale=1.0 - lam_init, lam_init=lam_init,
                          zero_fill=bool(zero_rows)),
        out_shape=out_shape,
        grid=(batch, nblk, nq),
        in_specs=in_specs,
        out_specs=out_specs,
        compiler_params=pltpu.CompilerParams(
            dimension_semantics=("arbitrary", "arbitrary", "arbitrary"),
            vmem_limit_bytes=VMEM_LIMIT),
        name="attn_diff" if diff else "attn_gqa",
    )(*args)


def _post_kernel(x_ref, oa_ref, ob_ref, cq_ref, gate_ref, mkv_ref, wa_ref, wb_ref, wc_ref,
                 wo_ref, g2_ref, wr_ref, br_ref, tri_ref,
                 xo_ref, h2_ref, route_ref, wt_ref, cnt_ref, carry_ref):
    tm = x_ref.shape[0]
    i = pl.program_id(0)

    @pl.when(i == 0)
    def _():
        carry_ref[...] = jnp.zeros_like(carry_ref)

    heads = []
    for hd in range(BRANCH_W // MX_HEAD_DIM):
        sl = slice(hd * MX_HEAD_DIM, (hd + 1) * MX_HEAD_DIM)
        sv = slice(BRANCH_W + hd * MX_HEAD_DIM, BRANCH_W + (hd + 1) * MX_HEAD_DIM)
        s = _dot_nt(cq_ref[:, sl], mkv_ref[:, sl])
        e = jnp.exp(s - jnp.max(s, axis=-1, keepdims=True))
        den = jnp.sum(e, axis=-1, keepdims=True)
        heads.append((_dot(e.astype(BF16), mkv_ref[:, sv]) / den).astype(BF16))
    oc = jnp.concatenate(heads, axis=1)

    ya = _dot(oa_ref[...], wa_ref[...])
    yb = _dot(ob_ref[...], wb_ref[...])
    yc = _dot(oc, wc_ref[...])
    merged = (gate_ref[:, 0:D_MODEL].astype(F32) * ya
              + gate_ref[:, D_MODEL:2 * D_MODEL].astype(F32) * yb
              + gate_ref[:, 2 * D_MODEL:3 * D_MODEL].astype(F32) * yc)
    xn = x_ref[...] + _dot(merged.astype(BF16), wo_ref[...])
    xo_ref[...] = xn
    h2 = _rms(xn, g2_ref[...])
    for c in range(CHUNKS):
        h2_ref[pl.ds(c, tm, stride=CHUNKS), :] = h2[:, c * LANES:(c + 1) * LANES]

    h_hi = h2.astype(BF16)
    h_lo = (h2 - h_hi.astype(F32)).astype(BF16)
    l2 = _dot_nt(wr_ref[...], h_hi)
    logits = (l2[:ROUTE_ROWS] + l2[ROUTE_ROWS:] + _dot_nt(wr_ref[:ROUTE_ROWS, :], h_lo)
              + br_ref[...])

    neg = -jnp.inf
    r8 = lax.broadcasted_iota(jnp.int32, (SUBLANES, tm), 0)
    r32 = lax.broadcasted_iota(jnp.int32, (N_EXPERTS, tm), 0)
    gl = jnp.where(r8 < N_GROUPS, logits[0:SUBLANES], neg)
    gmax = jnp.max(gl, axis=0, keepdims=True)
    gidx = jnp.min(jnp.where(gl == gmax, r8, SUBLANES), axis=0, keepdims=True)
    gp = 1.0 / jnp.sum(jnp.exp(gl - gmax), axis=0, keepdims=True)
    el = jnp.where((r32 // EXPERTS_PER_GROUP) == gidx, logits[SUBLANES:ROUTE_ROWS], neg)
    m1 = jnp.max(el, axis=0, keepdims=True)
    i1 = jnp.min(jnp.where(el == m1, r32, N_EXPERTS), axis=0, keepdims=True)
    el2 = jnp.where(r32 == i1, neg, el)
    m2 = jnp.max(el2, axis=0, keepdims=True)
    i2 = jnp.min(jnp.where(el2 == m2, r32, N_EXPERTS), axis=0, keepdims=True)
    d = jnp.exp(m2 - m1)
    w1 = gp / (1.0 + d)
    w2 = gp * d / (1.0 + d)

    hit1 = r32 == i1
    hit2 = r32 == i2
    oh = jnp.where(hit1 | hit2, 1.0, 0.0)
    before = _dot(oh.astype(BF16), tri_ref[...]) + carry_ref[...]
    rank1 = jnp.sum(jnp.where(hit1, before, 0.0), axis=0, keepdims=True)
    rank2 = jnp.sum(jnp.where(hit2, before, 0.0), axis=0, keepdims=True)
    carry_ref[...] = carry_ref[...] + jnp.sum(oh, axis=1, keepdims=True)
    zrow = jnp.zeros_like(w1)
    route = jnp.concatenate(
        [i1.astype(F32), i2.astype(F32), rank1, rank2, w1, w2, zrow, zrow], axis=0)
    route_ref[...] = route
    wt_ref[...] = jnp.concatenate(
        [route, jnp.zeros((LANES - SUBLANES, tm), F32)], axis=0).T
    cnt_ref[...] = jnp.broadcast_to(carry_ref[...], cnt_ref.shape)


def _post(x2d, oa, ob, cq, gate, mkv, wa, wb, wc, wo, g2, wr, br, tri):
    t = x2d.shape[0]
    tm = TM_POST
    nrb = SEQ // tm
    row = lambda i: (i, 0)
    const = lambda i: (0, 0)
    return pl.pallas_call(
        _post_kernel,
        out_shape=[jax.ShapeDtypeStruct((t, D_MODEL), F32),
                   jax.ShapeDtypeStruct((t * CHUNKS, LANES), F32),
                   jax.ShapeDtypeStruct((SUBLANES, t), F32),
                   jax.ShapeDtypeStruct((t, LANES), F32),
                   jax.ShapeDtypeStruct((N_EXPERTS, LANES), F32)],
        grid=(t // tm,),
        in_specs=[pl.BlockSpec((tm, D_MODEL), row),
                  pl.BlockSpec((tm, BRANCH_W), row),
                  pl.BlockSpec((tm, BRANCH_W), row),
                  pl.BlockSpec((tm, BRANCH_W), row),
                  pl.BlockSpec((tm, 3 * D_MODEL), row),
                  pl.BlockSpec((MEM_LEN, 2 * BRANCH_W), lambda i: (i // nrb, 0)),
                  pl.BlockSpec(wa.shape, const), pl.BlockSpec(wb.shape, const),
                  pl.BlockSpec(wc.shape, const), pl.BlockSpec(wo.shape, const),
                  pl.BlockSpec((1, D_MODEL), const),
                  pl.BlockSpec(wr.shape, const), pl.BlockSpec(br.shape, const),
                  pl.BlockSpec(tri.shape, const)],
        out_specs=[pl.BlockSpec((tm, D_MODEL), row),
                   pl.BlockSpec((tm * CHUNKS, LANES), row),
                   pl.BlockSpec((SUBLANES, tm), lambda i: (0, i)),
                   pl.BlockSpec((tm, LANES), row),
                   pl.BlockSpec((N_EXPERTS, LANES), const)],
        scratch_shapes=[pltpu.VMEM((N_EXPERTS, 1), F32)],
        compiler_params=pltpu.CompilerParams(
            dimension_semantics=("arbitrary",), vmem_limit_bytes=VMEM_LIMIT),
        name="post",
    )(x2d, oa, ob, cq, gate, mkv, wa, wb, wc, wo, g2, wr, br, tri)


def _plan_kernel(cnt_ref, route_ref, ltri_ref, slots_ref, plan_ref, *, bm):
    t = route_ref.shape[1]
    counts = cnt_ref[...]
    nblk_e = jnp.floor((counts + (bm - 1)) * (1.0 / bm))
    cum = _dot(ltri_ref[...], nblk_e.astype(BF16))
    first = cum - nblk_e
    first_row = first[:, 0:1] * bm
    e_tok = lax.broadcasted_iota(jnp.int32, (N_EXPERTS, t), 0)
    rows = []
    for k in range(TOP_K):
        e_k = route_ref[k:k + 1, :].astype(jnp.int32)
        base = jnp.sum(jnp.where(e_tok == e_k, first_row, 0.0), axis=0, keepdims=True)
        rows.append((base + route_ref[TOP_K + k:TOP_K + k + 1, :]).astype(jnp.int32))
    slots_ref[...] = jnp.concatenate(
        rows + [jnp.zeros((SUBLANES - TOP_K, t), jnp.int32)], axis=0)

    nb_used = cum[N_EXPERTS - 1:N_EXPERTS, :]
    b = lax.broadcasted_iota(jnp.int32, (1, LANES), 1).astype(F32)
    bidx = jnp.minimum(b, nb_used - 1.0)
    bexp = jnp.sum(jnp.where(cum <= bidx, 1.0, 0.0), axis=0, keepdims=True)
    sel = lax.broadcasted_iota(jnp.int32, (N_EXPERTS, LANES), 0).astype(F32) == bexp
    cnt_at = jnp.sum(jnp.where(sel, counts, 0.0), axis=0, keepdims=True)
    first_at = jnp.sum(jnp.where(sel, first, 0.0), axis=0, keepdims=True)
    bval = jnp.clip(cnt_at - (bidx - first_at) * bm, 0.0, float(bm))
    plan_ref[...] = jnp.concatenate(
        [bexp, bval, nb_used, jnp.zeros((SUBLANES - 3, LANES), F32)], axis=0).astype(jnp.int32)


def _plan(cnt, route, ltri, bm):
    t = route.shape[1]
    full = lambda shape: pl.BlockSpec(shape, lambda i: (0, 0))
    return pl.pallas_call(
        functools.partial(_plan_kernel, bm=bm),
        out_shape=[jax.ShapeDtypeStruct((SUBLANES, t), jnp.int32),
                   jax.ShapeDtypeStruct((SUBLANES, LANES), jnp.int32)],
        grid=(1,),
        in_specs=[full(cnt.shape), full(route.shape), full(ltri.shape)],
        out_specs=[full((SUBLANES, t)), full((SUBLANES, LANES))],
        compiler_params=pltpu.CompilerParams(
            dimension_semantics=("arbitrary",), vmem_limit_bytes=VMEM_LIMIT),
        name="plan",
    )(cnt, route, ltri)


def _start_row_gather(idx_refs, off, n, src_ref, dst_ref, slot, sem):
    def body(i, carry):
        for u in range(ISSUE_UNROLL // TOP_K):
            r = i * (ISSUE_UNROLL // TOP_K) + u
            for k in range(TOP_K):
                row = idx_refs[k][off + r]
                pltpu.make_async_copy(
                    src_ref.at[pl.ds(pl.multiple_of(row * CHUNKS, CHUNKS), CHUNKS)],
                    dst_ref.at[slot, pl.ds(pl.multiple_of((k * n + r) * CHUNKS, CHUNKS), CHUNKS)],
                    sem.at[slot]).start(priority=k)
        return carry
    lax.fori_loop(0, n // (ISSUE_UNROLL // TOP_K), body, 0)


def _wait_row_gather(n, src_ref, dst_ref, slot, sem):
    pltpu.make_async_copy(src_ref.at[pl.ds(0, n * CHUNKS)], dst_ref.at[slot], sem.at[slot]).wait()


def _rows_from_tiles(buf_ref, slot, first, n):
    return jnp.concatenate(
        [buf_ref[slot, pl.ds(first * CHUNKS + c, n, stride=CHUNKS), :] for c in range(CHUNKS)],
        axis=1)


def _dispatch_kernel(idx0_ref, idx1_ref, h2_ref, xin_ref, xbuf_ref, sem):
    del xin_ref
    td = idx0_ref.shape[0]
    idx_refs = (idx0_ref, idx1_ref)

    def body(i, carry):
        for u in range(ISSUE_UNROLL // TOP_K):
            tok = i * (ISSUE_UNROLL // TOP_K) + u
            for k in range(TOP_K):
                slot_row = idx_refs[k][tok]
                pltpu.make_async_copy(
                    h2_ref.at[pl.ds(pl.multiple_of(tok * CHUNKS, CHUNKS), CHUNKS)],
                    xbuf_ref.at[pl.ds(pl.multiple_of(slot_row * CHUNKS, CHUNKS), CHUNKS)],
                    sem.at[0]).start(priority=k)
        return carry
    lax.fori_loop(0, td // (ISSUE_UNROLL // TOP_K), body, 0)
    for _ in range(TOP_K):
        pltpu.make_async_copy(h2_ref, xbuf_ref.at[pl.ds(0, h2_ref.shape[0])], sem.at[0]).wait()


def _dispatch(slots, h2, xbuf_init):
    td = TD
    nsteps = slots.shape[0] // (TOP_K * td)
    return pl.pallas_call(
        _dispatch_kernel,
        out_shape=jax.ShapeDtypeStruct(xbuf_init.shape, F32),
        grid=(nsteps,),
        in_specs=[pl.BlockSpec((td,), lambda i: (i,), memory_space=pltpu.SMEM),
                  pl.BlockSpec((td,), lambda i: (nsteps + i,), memory_space=pltpu.SMEM),
                  pl.BlockSpec((td * CHUNKS, LANES), lambda i: (i, 0)),
                  pl.BlockSpec(memory_space=pl.ANY)],
        out_specs=pl.BlockSpec(memory_space=pl.ANY),
        input_output_aliases={3: 0},
        scratch_shapes=[pltpu.SemaphoreType.DMA((1,))],
        compiler_params=pltpu.CompilerParams(
            dimension_semantics=("arbitrary",), vmem_limit_bytes=VMEM_LIMIT),
        name="dispatch",
    )(slots, slots, h2, xbuf_init)


def _moe_kernel(bexp_ref, bval_ref, nb_ref, x_ref, wg_ref, wu_ref, wd_ref, y_ref,
                wgb, wub, wdb):
    b = pl.program_id(0)

    @pl.when((b == 0) | (bexp_ref[b] != bexp_ref[jnp.maximum(b - 1, 0)]))
    def _():
        wgb[...] = wg_ref[0, 0].astype(BF16)
        wub[...] = wu_ref[0, 0].astype(BF16)
        wdb[...] = wd_ref[0, 0].astype(BF16)

    @pl.when(b >= nb_ref[0])
    def _():
        y_ref[...] = jnp.zeros_like(y_ref)

    @pl.when(b < nb_ref[0])
    def _():
        bm = x_ref.shape[0] // CHUNKS
        xf = jnp.concatenate(
            [x_ref[pl.ds(c, bm, stride=CHUNKS), :] for c in range(CHUNKS)], axis=1)
        rowid = lax.broadcasted_iota(jnp.int32, (bm, 1), 0)
        xb = jnp.where(rowid < bval_ref[b], xf, 0.0).astype(BF16)
        hg = _dot(xb, wgb[...])
        hu = _dot(xb, wub[...])
        hid = (hg / (1.0 + jnp.exp(-hg)) * hu).astype(BF16)
        y = _dot(hid, wdb[...])
        for c in range(CHUNKS):
            y_ref[pl.ds(c, bm, stride=CHUNKS), :] = y[:, c * LANES:(c + 1) * LANES]


def _moe(bexp, bval, nb_used, xbuf, wg, wu, wd, layer, bm):
    nblk = bexp.shape[0]
    wmap = lambda b, bexp, bval, nb: (layer, bexp[b], 0, 0)
    xmap = lambda b, bexp, bval, nb: (jnp.minimum(b, nb[0] - 1), 0)
    return pl.pallas_call(
        _moe_kernel,
        out_shape=jax.ShapeDtypeStruct(xbuf.shape, F32),
        grid_spec=pltpu.PrefetchScalarGridSpec(
            num_scalar_prefetch=3,
            grid=(nblk,),
            in_specs=[
                pl.BlockSpec((bm * CHUNKS, LANES), xmap),
                pl.BlockSpec((1, 1, D_MODEL, D_EXPERT), wmap),
                pl.BlockSpec((1, 1, D_MODEL, D_EXPERT), wmap),
                pl.BlockSpec((1, 1, D_EXPERT, D_MODEL), wmap),
            ],
            out_specs=pl.BlockSpec((bm * CHUNKS, LANES), lambda b, bexp, bval, nb: (b, 0)),
            scratch_shapes=[pltpu.VMEM((D_MODEL, D_EXPERT), BF16),
                            pltpu.VMEM((D_MODEL, D_EXPERT), BF16),
                            pltpu.VMEM((D_EXPERT, D_MODEL), BF16)]),
        compiler_params=pltpu.CompilerParams(
            dimension_semantics=("arbitrary",), vmem_limit_bytes=VMEM_LIMIT),
        name="experts",
    )(bexp, bval, nb_used, xbuf, wg, wu, wd)


def _comb_kernel(cur0_ref, cur1_ref, nxt0_ref, nxt1_ref, x_ref, wt_ref, fg_ref, y_ref, o_ref,
                 ybuf, sem, *, final):
    tm = x_ref.shape[0]
    i = pl.program_id(0)
    n = pl.num_programs(0)
    slot = i % 2
    per = IDX_BLOCK // tm

    @pl.when(i == 0)
    def _():
        _start_row_gather((cur0_ref, cur1_ref), 0, tm, y_ref, ybuf, 0, sem)

    @pl.when(i + 1 < n)
    def _():
        _start_row_gather((nxt0_ref, nxt1_ref), ((i + 1) % per) * tm, tm, y_ref, ybuf,
                          1 - slot, sem)

    def compute(s):
        _wait_row_gather(2 * tm, y_ref, ybuf, s, sem)
        y0 = _rows_from_tiles(ybuf, s, 0, tm)
        y1 = _rows_from_tiles(ybuf, s, tm, tm)
        wt = wt_ref[...]
        xo = x_ref[...] + wt[:, 4:5] * y0 + wt[:, 5:6] * y1
        if final:
            xo = _rms(xo, fg_ref[...])
        o_ref[...] = xo

    for s in range(2):
        @pl.when(slot == s)
        def _(s=s):
            compute(s)


def _combine(dest, x2d, wt, fg, ybuf, *, final):
    t = x2d.shape[0]
    tm = TM_COMB
    nsteps = t // tm
    per = IDX_BLOCK // tm
    row = lambda i: (i, 0)
    smem = lambda index_map: pl.BlockSpec((IDX_BLOCK,), index_map, memory_space=pltpu.SMEM)
    return pl.pallas_call(
        functools.partial(_comb_kernel, final=final),
        out_shape=jax.ShapeDtypeStruct((t, D_MODEL), F32),
        grid=(nsteps,),
        in_specs=[
            smem(lambda i: (i // per,)),
            smem(lambda i: (t // IDX_BLOCK + i // per,)),
            smem(lambda i: (jnp.minimum(i + 1, nsteps - 1) // per,)),
            smem(lambda i: (t // IDX_BLOCK + jnp.minimum(i + 1, nsteps - 1) // per,)),
            pl.BlockSpec((tm, D_MODEL), row),
            pl.BlockSpec((tm, LANES), row),
            pl.BlockSpec((1, D_MODEL), lambda i: (0, 0)),
            pl.BlockSpec(memory_space=pl.ANY),
        ],
        out_specs=pl.BlockSpec((tm, D_MODEL), row),
        scratch_shapes=[pltpu.VMEM((2, 2 * tm * CHUNKS, LANES), F32),
                        pltpu.SemaphoreType.DMA((2,))],
        compiler_params=pltpu.CompilerParams(
            dimension_semantics=("arbitrary",), vmem_limit_bytes=VMEM_LIMIT),
        name="combine",
    )(dest, dest, dest, dest, x2d, wt, fg, ybuf)


def _rope_tables():
    pos = jnp.arange(SEQ, dtype=F32)
    inv = ROPE_THETA ** (-jnp.arange(0, HEAD_DIM, 2, dtype=F32) / HEAD_DIM)
    ang1 = pos[:, None] * inv[None, :]
    rows = SEQ // GRID_W
    r = jnp.broadcast_to(jnp.arange(rows, dtype=F32)[:, None], (rows, GRID_W)).reshape(-1)
    c = jnp.broadcast_to(jnp.arange(GRID_W, dtype=F32)[None, :], (rows, GRID_W)).reshape(-1)
    axis_dim = HEAD_DIM // 2
    inv2 = ROPE_THETA ** (-jnp.arange(0, axis_dim, 2, dtype=F32) / axis_dim)
    ang2 = jnp.concatenate([r[:, None] * inv2, c[:, None] * inv2], axis=-1)

    def tables(ang):
        cs, sn = jnp.cos(ang), jnp.sin(ang)
        reps = LANES // HEAD_DIM
        return (jnp.tile(jnp.concatenate([cs, cs], axis=1), (1, reps)),
                jnp.tile(jnp.concatenate([-sn, sn], axis=1), (1, reps)))

    return tables(ang1) + tables(ang2)


def kernel(x, mem, mem_norm_g, w_mem_kv, norm1_g, w_in, lam_q1, lam_k1, lam_q2, lam_k2,
           subln_g, q_norm_g, k_norm_g, w_up_a, w_up_b, w_up_c, w_out, norm2_g,
           w_router_group, b_router_group, w_router_expert, b_router_expert,
           w_exp_gate, w_exp_up, w_exp_down, final_norm_g):
    batch, seq, d = x.shape
    depth = w_in.shape[0]
    assert (seq, d, mem.shape[1]) == (SEQ, D_MODEL, MEM_LEN)
    t = batch * seq
    ca, sa, cb, sb = _rope_tables()
    gidx = jnp.arange(512) // HEAD_DIM
    bd = (gidx[:, None] == gidx[None, :]).astype(BF16)
    ti = jnp.arange(TM_POST)
    tri = (ti[:, None] < ti[None, :]).astype(BF16)
    ei = jnp.arange(N_EXPERTS)
    ltri = (ei[None, :] <= ei[:, None]).astype(BF16)
    buf_rows = t * TOP_K + N_EXPERTS * BM
    assert buf_rows // BM <= LANES

    mkv = _memkv(mem.reshape(batch * MEM_LEN, d), mem_norm_g.reshape(1, d), w_mem_kv.astype(BF16))
    x2d = x.reshape(t, d)
    ybuf = None
    for l in range(depth):
        wl = w_in[l]
        bk0, bk1 = wl[:, 2048:2112], wl[:, 2112:2176]
        bv0, bv1 = wl[:, 2176:2240], wl[:, 2240:2304]
        w = jnp.concatenate([wl[:, :2048], bk0, bk0, bk1, bk1, bv0, bv0, bv1, bv1, wl[:, 2304:]],
                            axis=1).astype(BF16)
        aq, ak, av, bq, bk, bv, cq, gate = _in_proj(
            x2d, norm1_g[l].reshape(1, d), w, ca, sa, cb, sb,
            jnp.tile(q_norm_g[l], 8).reshape(1, 512), jnp.tile(k_norm_g[l], 4).reshape(1, 256), bd)

        lam_init = 0.8 - 0.6 * math.exp(-0.3 * l)
        lamp = jnp.stack([lam_q1[l], lam_k1[l], lam_q2[l], lam_k2[l]]).astype(F32)
        oa = _attention(aq, ak, av, batch, diff=True, lamp=lamp,
                        gs=subln_g[l].reshape(LANES, 1), lam_init=lam_init)
        if ybuf is None:
            ob, ybuf = _attention(bq, bk, bv, batch, diff=False, zero_rows=buf_rows * CHUNKS)
        else:
            ob = _attention(bq, bk, bv, batch, diff=False)

        wr = jnp.zeros((ROUTE_ROWS, d), F32)
        wr = wr.at[0:N_GROUPS].set(w_router_group[l].T).at[SUBLANES:].set(w_router_expert[l].T)
        wr_hi = wr.astype(BF16)
        wr_lo = (wr - wr_hi.astype(F32)).astype(BF16)
        br = jnp.zeros((ROUTE_ROWS, 1), F32)
        br = br.at[0:N_GROUPS, 0].set(b_router_group[l]).at[SUBLANES:, 0].set(b_router_expert[l])
        x2d, h2, route, wt, cnt = _post(
            x2d, oa, ob, cq, gate, mkv, w_up_a[l].astype(BF16), w_up_b[l].astype(BF16),
            w_up_c[l].astype(BF16), w_out[l].astype(BF16), norm2_g[l].reshape(1, d),
            jnp.concatenate([wr_hi, wr_lo], axis=0), br, tri)

        slots, plan = _plan(cnt, route, ltri, BM)
        slots = slots[:TOP_K].reshape(-1)
        nblk = buf_rows // BM
        xbuf = _dispatch(slots, h2, ybuf)
        ybuf = _moe(plan[0, :nblk], plan[1, :nblk], plan[2, :1], xbuf,
                    w_exp_gate, w_exp_up, w_exp_down, l, BM)
        x2d = _combine(slots, x2d, wt, final_norm_g.reshape(1, d), ybuf,
                       final=(l == depth - 1))
    return x2d.reshape(batch, seq, d)
```

```python
import functools
import math

import jax
import jax.numpy as jnp
from jax import lax
from jax.experimental import pallas as pl
from jax.experimental.pallas import tpu as pltpu

F32 = jnp.float32
BF16 = jnp.bfloat16

D_MODEL = 1024
SEQ = 2048
MEM_LEN = 256
HEAD_DIM = 64
MX_HEAD_DIM = 128
BRANCH_W = 512
GRID_W = 64
ROPE_THETA = 10000.0
NORM_EPS = 1e-6
N_GROUPS = 4
EXPERTS_PER_GROUP = 8
N_EXPERTS = N_GROUPS * EXPERTS_PER_GROUP
TOP_K = 2
D_EXPERT = 512

LANES = 128
SUBLANES = 8
CHUNKS = D_MODEL // LANES

TM_IN = 512
TQ = 1024
TK = 256
ONES_ROWS = 16
LOG2E = math.log2(math.e)
TM_POST = 512
BM = 512
TM_COMB = 256
TD = 1024
ISSUE_UNROLL = 16
IDX_BLOCK = 1024
VMEM_LIMIT = 56 * 1024 * 1024

C_AQ, C_AK, C_AV, C_BQ, C_BK, C_BV, C_CQ, C_G, C_END = (
    0, 512, 1024, 1536, 2048, 2176, 2304, 2816, 5888)
ROUTE_ROWS = 40


def _rms(xf, g):
    ms = jnp.mean(xf * xf, axis=-1, keepdims=True)
    return xf * lax.rsqrt(ms + NORM_EPS) * g


def _dot(a, b):
    return jnp.dot(a, b, preferred_element_type=F32)


def _dot_nt(a, b):
    return lax.dot_general(a, b, (((1,), (1,)), ((), ())), preferred_element_type=F32)


def _memkv_kernel(m_ref, g_ref, w_ref, o_ref):
    h = _rms(m_ref[...], g_ref[...]).astype(BF16)
    o_ref[...] = _dot(h, w_ref[...]).astype(BF16)


def _memkv(mem2d, g, w):
    n = mem2d.shape[0]
    tm = 512
    return pl.pallas_call(
        _memkv_kernel,
        out_shape=jax.ShapeDtypeStruct((n, w.shape[1]), BF16),
        grid=(n // tm,),
        in_specs=[pl.BlockSpec((tm, D_MODEL), lambda i: (i, 0)),
                  pl.BlockSpec((1, D_MODEL), lambda i: (0, 0)),
                  pl.BlockSpec(w.shape, lambda i: (0, 0))],
        out_specs=pl.BlockSpec((tm, w.shape[1]), lambda i: (i, 0)),
        compiler_params=pltpu.CompilerParams(
            dimension_semantics=("arbitrary",), vmem_limit_bytes=VMEM_LIMIT),
        name="memkv",
    )(mem2d, g, w)


def _in_kernel(x_ref, g1_ref, w_ref, ca_ref, sa_ref, cb_ref, sb_ref, qg_ref, kg_ref, bd_ref,
               aq_ref, ak_ref, av_ref, bq_ref, bk_ref, bv_ref, cq_ref, gate_ref):
    tm = x_ref.shape[0]
    h = _rms(x_ref[...], g1_ref[...]).astype(BF16)
    lane = lax.broadcasted_iota(jnp.int32, (tm, LANES), 1)
    first_half = (lane & (HEAD_DIM // 2)) == 0

    def seg(lo, hi):
        return _dot(h, w_ref[:, lo:hi])

    def rope(p, c, s):
        sw = jnp.where(first_half, pltpu.roll(p, LANES - HEAD_DIM // 2, 1),
                       pltpu.roll(p, HEAD_DIM // 2, 1))
        return p * c + sw * s

    def group_norm(p, gain):
        n = p.shape[1]
        ss = _dot((p * p).astype(BF16), bd_ref[:n, :n])
        return p * lax.rsqrt(ss * (1.0 / HEAD_DIM) + NORM_EPS) * gain

    def rope_store(p, c_ref, s_ref, o_ref, scale):
        c = c_ref[...]
        s = s_ref[...]
        for j in range(p.shape[1] // LANES):
            sl = slice(j * LANES, (j + 1) * LANES)
            o_ref[:, sl] = (rope(p[:, sl], c, s) * scale).astype(BF16)

    q_scale = HEAD_DIM ** -0.5 * LOG2E
    rope_store(seg(C_AQ, C_AK), ca_ref, sa_ref, aq_ref, q_scale)
    rope_store(seg(C_AK, C_AV), ca_ref, sa_ref, ak_ref, 1.0)
    av_ref[...] = seg(C_AV, C_BQ).T.astype(BF16)
    rope_store(group_norm(seg(C_BQ, C_BK), qg_ref[...]), cb_ref, sb_ref, bq_ref, q_scale)

    lo_half = lane < HEAD_DIM

    def both_halves(p):
        sw = pltpu.roll(p, HEAD_DIM, 1)
        return jnp.where(lo_half, p, sw), jnp.where(lo_half, sw, p)

    kb = rope(group_norm(seg(C_BK, C_BV), kg_ref[...]), cb_ref[...], sb_ref[...])
    for j, kj in enumerate(both_halves(kb)):
        bk_ref[:, j * LANES:(j + 1) * LANES] = kj.astype(BF16)
    for j, vj in enumerate(both_halves(seg(C_BV, C_CQ))):
        bv_ref[j * LANES:(j + 1) * LANES, :] = vj.T.astype(BF16)
    cq_ref[...] = (seg(C_CQ, C_G) * (MX_HEAD_DIM ** -0.5)).astype(BF16)
    for j in range((C_END - C_G) // 512):
        lo = C_G + j * 512
        z = seg(lo, lo + 512)
        gate_ref[:, j * 512:(j + 1) * 512] = (1.0 / (1.0 + jnp.exp(-z))).astype(BF16)


def _in_proj(x2d, g1, w, ca, sa, cb, sb, qg, kg, bd):
    t = x2d.shape[0]
    tm = TM_IN
    nrb = SEQ // tm
    row = lambda i: (i, 0)
    const = lambda i: (0, 0)
    tab = lambda i: (i % nrb, 0)
    outs = ((512, False), (512, False), (512, True), (512, False), (256, False), (256, True),
            (512, False), (3072, False))
    col = lambda i: (0, i)
    return pl.pallas_call(
        _in_kernel,
        out_shape=[jax.ShapeDtypeStruct((n, t) if tr else (t, n), BF16) for n, tr in outs],
        grid=(t // tm,),
        in_specs=[pl.BlockSpec((tm, D_MODEL), row),
                  pl.BlockSpec((1, D_MODEL), const),
                  pl.BlockSpec(w.shape, const),
                  pl.BlockSpec((tm, LANES), tab), pl.BlockSpec((tm, LANES), tab),
                  pl.BlockSpec((tm, LANES), tab), pl.BlockSpec((tm, LANES), tab),
                  pl.BlockSpec((1, 512), const), pl.BlockSpec((1, LANES), const),
                  pl.BlockSpec((512, 512), const)],
        out_specs=[pl.BlockSpec((n, tm), col) if tr else pl.BlockSpec((tm, n), row)
                   for n, tr in outs],
        compiler_params=pltpu.CompilerParams(
            dimension_semantics=("arbitrary",), vmem_limit_bytes=VMEM_LIMIT),
        name="in_proj",
    )(x2d, g1, w, ca, sa, cb, sb, qg, kg, bd)


def _attn_kernel(*refs, diff, post_scale, lam_init, zero_fill=False):
    if diff:
        lamp_ref, gs_ref, q_ref, k_ref, vt_ref, o_ref = refs
    elif zero_fill:
        q_ref, k_ref, vt_ref, o_ref, z_ref = refs
        z_ref[...] = jnp.zeros_like(z_ref)
    else:
        q_ref, k_ref, vt_ref, o_ref = refs
    tq = q_ref.shape[0]
    q = q_ref[...]
    lane = lax.broadcasted_iota(jnp.int32, (tq, LANES), 1)
    lo = lane < HEAD_DIM
    zero = jnp.zeros_like(q)
    qs = jnp.concatenate([jnp.where(lo, q, zero), jnp.where(lo, zero, q)], axis=0)
    cols = 2 * tq
    ones = jnp.ones((ONES_ROWS, TK), BF16)
    m = jnp.full((1, cols), -jnp.inf, F32)
    acc = jnp.zeros((LANES + ONES_ROWS, cols), F32)
    nchunks = SEQ // TK

    def scores(j):
        return _dot_nt(k_ref[j * TK:(j + 1) * TK, :], qs)

    st_next = scores(0)
    for j in range(nchunks):
        st = st_next
        if j + 1 < nchunks:
            st_next = scores(j + 1)
        vtj = jnp.concatenate([vt_ref[:, j * TK:(j + 1) * TK], ones], axis=0)
        m_new = jnp.maximum(m, jnp.max(st, axis=0, keepdims=True))
        alpha = jnp.exp2(m - m_new)
        e = jnp.exp2(st - m_new).astype(BF16)
        acc = alpha * acc + _dot(vtj, e)
        m = m_new
    o = acc[:LANES] / acc[LANES:LANES + 1]
    if diff:
        lp = lamp_ref[...]
        lam = (jnp.exp(jnp.sum(lp[0:1] * lp[1:2], axis=-1, keepdims=True))
               - jnp.exp(jnp.sum(lp[2:3] * lp[3:4], axis=-1, keepdims=True)) + lam_init)
        d = o[:, :tq] - lam * o[:, tq:]
        ms = jnp.mean(d * d, axis=0, keepdims=True)
        out_t = d * lax.rsqrt(ms + NORM_EPS) * gs_ref[...] * post_scale
    else:
        row = lax.broadcasted_iota(jnp.int32, (LANES, tq), 0)
        out_t = jnp.where(row < HEAD_DIM, o[:, :tq], o[:, tq:])
    o_ref[...] = out_t.T.astype(BF16)


def _attention(q, k, vt, batch, *, diff, lamp=None, gs=None, lam_init=0.0, zero_rows=0):
    t = q.shape[0]
    nq = SEQ // TQ
    nblk = q.shape[1] // LANES
    kv_per = nblk // (k.shape[1] // LANES)
    qmap = lambda b, h, i: (b * nq + i, h)
    out_shape = jax.ShapeDtypeStruct((t, q.shape[1]), BF16)
    out_specs = pl.BlockSpec((TQ, LANES), qmap)
    if zero_rows:
        zblk = zero_rows // (batch * nblk * nq)
        out_shape = [out_shape, jax.ShapeDtypeStruct((zero_rows, LANES), F32)]
        out_specs = [out_specs,
                     pl.BlockSpec((zblk, LANES), lambda b, h, i: ((b * nblk + h) * nq + i, 0))]
    in_specs = [pl.BlockSpec((TQ, LANES), qmap),
                pl.BlockSpec((SEQ, LANES), lambda b, h, i: (b, h // kv_per)),
                pl.BlockSpec((LANES, SEQ), lambda b, h, i: (h // kv_per, b))]
    args = [q, k, vt]
    if diff:
        const = lambda b, h, i: (0, 0)
        in_specs = [pl.BlockSpec((4, HEAD_DIM), const), pl.BlockSpec((LANES, 1), const)] + in_specs
        args = [lamp, gs] + args
    return pl.pallas_call(
        functools.partial(_attn_kernel, diff=diff, post_scale=1.0 - lam_init, lam_init=lam_init,
                          zero_fill=bool(zero_rows)),
        out_shape=out_shape,
        grid=(batch, nblk, nq),
        in_specs=in_specs,
        out_specs=out_specs,
        compiler_params=pltpu.CompilerParams(
            dimension_semantics=("arbitrary", "arbitrary", "arbitrary"),
            vmem_limit_bytes=VMEM_LIMIT),
        name="attn_diff" if diff else "attn_gqa",
    )(*args)


def _post_kernel(x_ref, oa_ref, ob_ref, cq_ref, gate_ref, mkv_ref, wa_ref, wb_ref, wc_ref,
                 wo_ref, g2_ref, wr_ref, br_ref, tri_ref,
                 xo_ref, h2_ref, route_ref, wt_ref, cnt_ref, carry_ref):
    tm = x_ref.shape[0]
    i = pl.program_id(0)

    @pl.when(i == 0)
    def _():
        carry_ref[...] = jnp.zeros_like(carry_ref)

    heads = []
    for hd in range(BRANCH_W // MX_HEAD_DIM):
        sl = slice(hd * MX_HEAD_DIM, (hd + 1) * MX_HEAD_DIM)
        sv = slice(BRANCH_W + hd * MX_HEAD_DIM, BRANCH_W + (hd + 1) * MX_HEAD_DIM)
        s = _dot_nt(cq_ref[:, sl], mkv_ref[:, sl])
        e = jnp.exp(s - jnp.max(s, axis=-1, keepdims=True))
        den = jnp.sum(e, axis=-1, keepdims=True)
        heads.append((_dot(e.astype(BF16), mkv_ref[:, sv]) / den).astype(BF16))
    oc = jnp.concatenate(heads, axis=1)

    ya = _dot(oa_ref[...], wa_ref[...])
    yb = _dot(ob_ref[...], wb_ref[...])
    yc = _dot(oc, wc_ref[...])
    merged = (gate_ref[:, 0:D_MODEL].astype(F32) * ya
              + gate_ref[:, D_MODEL:2 * D_MODEL].astype(F32) * yb
              + gate_ref[:, 2 * D_MODEL:3 * D_MODEL].astype(F32) * yc)
    xn = x_ref[...] + _dot(merged.astype(BF16), wo_ref[...])
    xo_ref[...] = xn
    h2 = _rms(xn, g2_ref[...])
    for c in range(CHUNKS):
        h2_ref[pl.ds(c, tm, stride=CHUNKS), :] = h2[:, c * LANES:(c + 1) * LANES]

    h_hi = h2.astype(BF16)
    h_lo = (h2 - h_hi.astype(F32)).astype(BF16)
    l2 = _dot_nt(wr_ref[...], h_hi)
    logits = (l2[:ROUTE_ROWS] + l2[ROUTE_ROWS:] + _dot_nt(wr_ref[:ROUTE_ROWS, :], h_lo)
              + br_ref[...])

    neg = -jnp.inf
    r8 = lax.broadcasted_iota(jnp.int32, (SUBLANES, tm), 0)
    r32 = lax.broadcasted_iota(jnp.int32, (N_EXPERTS, tm), 0)
    gl = jnp.where(r8 < N_GROUPS, logits[0:SUBLANES], neg)
    gmax = jnp.max(gl, axis=0, keepdims=True)
    gidx = jnp.min(jnp.where(gl == gmax, r8, SUBLANES), axis=0, keepdims=True)
    gp = 1.0 / jnp.sum(jnp.exp(gl - gmax), axis=0, keepdims=True)
    el = jnp.where((r32 // EXPERTS_PER_GROUP) == gidx, logits[SUBLANES:ROUTE_ROWS], neg)
    m1 = jnp.max(el, axis=0, keepdims=True)
    i1 = jnp.min(jnp.where(el == m1, r32, N_EXPERTS), axis=0, keepdims=True)
    el2 = jnp.where(r32 == i1, neg, el)
    m2 = jnp.max(el2, axis=0, keepdims=True)
    i2 = jnp.min(jnp.where(el2 == m2, r32, N_EXPERTS), axis=0, keepdims=True)
    d = jnp.exp(m2 - m1)
    w1 = gp / (1.0 + d)
    w2 = gp * d / (1.0 + d)

    hit1 = r32 == i1
    hit2 = r32 == i2
    oh = jnp.where(hit1 | hit2, 1.0, 0.0)
    before = _dot(oh.astype(BF16), tri_ref[...]) + carry_ref[...]
    rank1 = jnp.sum(jnp.where(hit1, before, 0.0), axis=0, keepdims=True)
    rank2 = jnp.sum(jnp.where(hit2, before, 0.0), axis=0, keepdims=True)
    carry_ref[...] = carry_ref[...] + jnp.sum(oh, axis=1, keepdims=True)
    zrow = jnp.zeros_like(w1)
    route = jnp.concatenate(
        [i1.astype(F32), i2.astype(F32), rank1, rank2, w1, w2, zrow, zrow], axis=0)
    route_ref[...] = route
    wt_ref[...] = jnp.concatenate(
        [route, jnp.zeros((LANES - SUBLANES, tm), F32)], axis=0).T
    cnt_ref[...] = jnp.broadcast_to(carry_ref[...], cnt_ref.shape)


def _post(x2d, oa, ob, cq, gate, mkv, wa, wb, wc, wo, g2, wr, br, tri):
    t = x2d.shape[0]
    tm = TM_POST
    nrb = SEQ // tm
    row = lambda i: (i, 0)
    const = lambda i: (0, 0)
    return pl.pallas_call(
        _post_kernel,
        out_shape=[jax.ShapeDtypeStruct((t, D_MODEL), F32),
                   jax.ShapeDtypeStruct((t * CHUNKS, LANES), F32),
                   jax.ShapeDtypeStruct((SUBLANES, t), F32),
                   jax.ShapeDtypeStruct((t, LANES), F32),
                   jax.ShapeDtypeStruct((N_EXPERTS, LANES), F32)],
        grid=(t // tm,),
        in_specs=[pl.BlockSpec((tm, D_MODEL), row),
                  pl.BlockSpec((tm, BRANCH_W), row),
                  pl.BlockSpec((tm, BRANCH_W), row),
                  pl.BlockSpec((tm, BRANCH_W), row),
                  pl.BlockSpec((tm, 3 * D_MODEL), row),
                  pl.BlockSpec((MEM_LEN, 2 * BRANCH_W), lambda i: (i // nrb, 0)),
                  pl.BlockSpec(wa.shape, const), pl.BlockSpec(wb.shape, const),
                  pl.BlockSpec(wc.shape, const), pl.BlockSpec(wo.shape, const),
                  pl.BlockSpec((1, D_MODEL), const),
                  pl.BlockSpec(wr.shape, const), pl.BlockSpec(br.shape, const),
                  pl.BlockSpec(tri.shape, const)],
        out_specs=[pl.BlockSpec((tm, D_MODEL), row),
                   pl.BlockSpec((tm * CHUNKS, LANES), row),
                   pl.BlockSpec((SUBLANES, tm), lambda i: (0, i)),
                   pl.BlockSpec((tm, LANES), row),
                   pl.BlockSpec((N_EXPERTS, LANES), const)],
        scratch_shapes=[pltpu.VMEM((N_EXPERTS, 1), F32)],
        compiler_params=pltpu.CompilerParams(
            dimension_semantics=("arbitrary",), vmem_limit_bytes=VMEM_LIMIT),
        name="post",
    )(x2d, oa, ob, cq, gate, mkv, wa, wb, wc, wo, g2, wr, br, tri)


def _plan_kernel(cnt_ref, route_ref, ltri_ref, slots_ref, plan_ref, *, bm):
    t = route_ref.shape[1]
    counts = cnt_ref[...]
    nblk_e = jnp.floor((counts + (bm - 1)) * (1.0 / bm))
    cum = _dot(ltri_ref[...], nblk_e.astype(BF16))
    first = cum - nblk_e
    first_row = first[:, 0:1] * bm
    e_tok = lax.broadcasted_iota(jnp.int32, (N_EXPERTS, t), 0)
    rows = []
    for k in range(TOP_K):
        e_k = route_ref[k:k + 1, :].astype(jnp.int32)
        base = jnp.sum(jnp.where(e_tok == e_k, first_row, 0.0), axis=0, keepdims=True)
        rows.append((base + route_ref[TOP_K + k:TOP_K + k + 1, :]).astype(jnp.int32))
    slots_ref[...] = jnp.concatenate(
        rows + [jnp.zeros((SUBLANES - TOP_K, t), jnp.int32)], axis=0)

    nb_used = cum[N_EXPERTS - 1:N_EXPERTS, :]
    b = lax.broadcasted_iota(jnp.int32, (1, LANES), 1).astype(F32)
    bidx = jnp.minimum(b, nb_used - 1.0)
    bexp = jnp.sum(jnp.where(cum <= bidx, 1.0, 0.0), axis=0, keepdims=True)
    sel = lax.broadcasted_iota(jnp.int32, (N_EXPERTS, LANES), 0).astype(F32) == bexp
    cnt_at = jnp.sum(jnp.where(sel, counts, 0.0), axis=0, keepdims=True)
    first_at = jnp.sum(jnp.where(sel, first, 0.0), axis=0, keepdims=True)
    bval = jnp.clip(cnt_at - (bidx - first_at) * bm, 0.0, float(bm))
    plan_ref[...] = jnp.concatenate(
        [bexp, bval, nb_used, jnp.zeros((SUBLANES - 3, LANES), F32)], axis=0).astype(jnp.int32)


def _plan(cnt, route, ltri, bm):
    t = route.shape[1]
    full = lambda shape: pl.BlockSpec(shape, lambda i: (0, 0))
    return pl.pallas_call(
        functools.partial(_plan_kernel, bm=bm),
        out_shape=[jax.ShapeDtypeStruct((SUBLANES, t), jnp.int32),
                   jax.ShapeDtypeStruct((SUBLANES, LANES), jnp.int32)],
        grid=(1,),
        in_specs=[full(cnt.shape), full(route.shape), full(ltri.shape)],
        out_specs=[full((SUBLANES, t)), full((SUBLANES, LANES))],
        compiler_params=pltpu.CompilerParams(
            dimension_semantics=("arbitrary",), vmem_limit_bytes=VMEM_LIMIT),
        name="plan",
    )(cnt, route, ltri)


def _start_row_gather(idx_refs, off, n, src_ref, dst_ref, slot, sem):
    def body(i, carry):
        for u in range(ISSUE_UNROLL // TOP_K):
            r = i * (ISSUE_UNROLL // TOP_K) + u
            for k in range(TOP_K):
                row = idx_refs[k][off + r]
                pltpu.make_async_copy(
                    src_ref.at[pl.ds(pl.multiple_of(row * CHUNKS, CHUNKS), CHUNKS)],
                    dst_ref.at[slot, pl.ds(pl.multiple_of((k * n + r) * CHUNKS, CHUNKS), CHUNKS)],
                    sem.at[slot]).start(priority=k)
        return carry
    lax.fori_loop(0, n // (ISSUE_UNROLL // TOP_K), body, 0)


def _wait_row_gather(n, src_ref, dst_ref, slot, sem):
    pltpu.make_async_copy(src_ref.at[pl.ds(0, n * CHUNKS)], dst_ref.at[slot], sem.at[slot]).wait()


def _rows_from_tiles(buf_ref, slot, first, n):
    return jnp.concatenate(
        [buf_ref[slot, pl.ds(first * CHUNKS + c, n, stride=CHUNKS), :] for c in range(CHUNKS)],
        axis=1)


def _dispatch_kernel(idx0_ref, idx1_ref, h2_ref, xin_ref, xbuf_ref, sem):
    del xin_ref
    td = idx0_ref.shape[0]
    idx_refs = (idx0_ref, idx1_ref)

    def body(i, carry):
        for u in range(ISSUE_UNROLL // TOP_K):
            tok = i * (ISSUE_UNROLL // TOP_K) + u
            for k in range(TOP_K):
                slot_row = idx_refs[k][tok]
                pltpu.make_async_copy(
                    h2_ref.at[pl.ds(pl.multiple_of(tok * CHUNKS, CHUNKS), CHUNKS)],
                    xbuf_ref.at[pl.ds(pl.multiple_of(slot_row * CHUNKS, CHUNKS), CHUNKS)],
                    sem.at[0]).start(priority=k)
        return carry
    lax.fori_loop(0, td // (ISSUE_UNROLL // TOP_K), body, 0)
    for _ in range(TOP_K):
        pltpu.make_async_copy(h2_ref, xbuf_ref.at[pl.ds(0, h2_ref.shape[0])], sem.at[0]).wait()


def _dispatch(slots, h2, xbuf_init):
    td = TD
    nsteps = slots.shape[0] // (TOP_K * td)
    return pl.pallas_call(
        _dispatch_kernel,
        out_shape=jax.ShapeDtypeStruct(xbuf_init.shape, F32),
        grid=(nsteps,),
        in_specs=[pl.BlockSpec((td,), lambda i: (i,), memory_space=pltpu.SMEM),
                  pl.BlockSpec((td,), lambda i: (nsteps + i,), memory_space=pltpu.SMEM),
                  pl.BlockSpec((td * CHUNKS, LANES), lambda i: (i, 0)),
                  pl.BlockSpec(memory_space=pl.ANY)],
        out_specs=pl.BlockSpec(memory_space=pl.ANY),
        input_output_aliases={3: 0},
        scratch_shapes=[pltpu.SemaphoreType.DMA((1,))],
        compiler_params=pltpu.CompilerParams(
            dimension_semantics=("arbitrary",), vmem_limit_bytes=VMEM_LIMIT),
        name="dispatch",
    )(slots, slots, h2, xbuf_init)


def _moe_kernel(bexp_ref, bval_ref, nb_ref, x_ref, wg_ref, wu_ref, wd_ref, y_ref,
                wgb, wub, wdb):
    b = pl.program_id(0)

    @pl.when((b == 0) | (bexp_ref[b] != bexp_ref[jnp.maximum(b - 1, 0)]))
    def _():
        wgb[...] = wg_ref[0, 0].astype(BF16)
        wub[...] = wu_ref[0, 0].astype(BF16)
        wdb[...] = wd_ref[0, 0].astype(BF16)

    @pl.when(b >= nb_ref[0])
    def _():
        y_ref[...] = jnp.zeros_like(y_ref)

    @pl.when(b < nb_ref[0])
    def _():
        bm = x_ref.shape[0] // CHUNKS
        xf = jnp.concatenate(
            [x_ref[pl.ds(c, bm, stride=CHUNKS), :] for c in range(CHUNKS)], axis=1)
        rowid = lax.broadcasted_iota(jnp.int32, (bm, 1), 0)
        xb = jnp.where(rowid < bval_ref[b], xf, 0.0).astype(BF16)
        hg = _dot(xb, wgb[...])
        hu = _dot(xb, wub[...])
        hid = (hg / (1.0 + jnp.exp(-hg)) * hu).astype(BF16)
        y = _dot(hid, wdb[...])
        for c in range(CHUNKS):
            y_ref[pl.ds(c, bm, stride=CHUNKS), :] = y[:, c * LANES:(c + 1) * LANES]


def _moe(bexp, bval, nb_used, xbuf, wg, wu, wd, layer, bm):
    nblk = bexp.shape[0]
    wmap = lambda b, bexp, bval, nb: (layer, bexp[b], 0, 0)
    xmap = lambda b, bexp, bval, nb: (jnp.minimum(b, nb[0] - 1), 0)
    return pl.pallas_call(
        _moe_kernel,
        out_shape=jax.ShapeDtypeStruct(xbuf.shape, F32),
        grid_spec=pltpu.PrefetchScalarGridSpec(
            num_scalar_prefetch=3,
            grid=(nblk,),
            in_specs=[
                pl.BlockSpec((bm * CHUNKS, LANES), xmap),
                pl.BlockSpec((1, 1, D_MODEL, D_EXPERT), wmap),
                pl.BlockSpec((1, 1, D_MODEL, D_EXPERT), wmap),
                pl.BlockSpec((1, 1, D_EXPERT, D_MODEL), wmap),
            ],
            out_specs=pl.BlockSpec((bm * CHUNKS, LANES), lambda b, bexp, bval, nb: (b, 0)),
            scratch_shapes=[pltpu.VMEM((D_MODEL, D_EXPERT), BF16),
                            pltpu.VMEM((D_MODEL, D_EXPERT), BF16),
                            pltpu.VMEM((D_EXPERT, D_MODEL), BF16)]),
        compiler_params=pltpu.CompilerParams(
            dimension_semantics=("arbitrary",), vmem_limit_bytes=VMEM_LIMIT),
        name="experts",
    )(bexp, bval, nb_used, xbuf, wg, wu, wd)


def _comb_kernel(cur0_ref, cur1_ref, nxt0_ref, nxt1_ref, x_ref, wt_ref, fg_ref, y_ref, o_ref,
                 ybuf, sem, *, final):
    tm = x_ref.shape[0]
    i = pl.program_id(0)
    n = pl.num_programs(0)
    slot = i % 2
    per = IDX_BLOCK // tm

    @pl.when(i == 0)
    def _():
        _start_row_gather((cur0_ref, cur1_ref), 0, tm, y_ref, ybuf, 0, sem)

    @pl.when(i + 1 < n)
    def _():
        _start_row_gather((nxt0_ref, nxt1_ref), ((i + 1) % per) * tm, tm, y_ref, ybuf,
                          1 - slot, sem)

    def compute(s):
        _wait_row_gather(2 * tm, y_ref, ybuf, s, sem)
        y0 = _rows_from_tiles(ybuf, s, 0, tm)
        y1 = _rows_from_tiles(ybuf, s, tm, tm)
        wt = wt_ref[...]
        xo = x_ref[...] + wt[:, 4:5] * y0 + wt[:, 5:6] * y1
        if final:
            xo = _rms(xo, fg_ref[...])
        o_ref[...] = xo

    for s in range(2):
        @pl.when(slot == s)
        def _(s=s):
            compute(s)


def _combine(dest, x2d, wt, fg, ybuf, *, final):
    t = x2d.shape[0]
    tm = TM_COMB
    nsteps = t // tm
    per = IDX_BLOCK // tm
    row = lambda i: (i, 0)
    smem = lambda index_map: pl.BlockSpec((IDX_BLOCK,), index_map, memory_space=pltpu.SMEM)
    return pl.pallas_call(
        functools.partial(_comb_kernel, final=final),
        out_shape=jax.ShapeDtypeStruct((t, D_MODEL), F32),
        grid=(nsteps,),
        in_specs=[
            smem(lambda i: (i // per,)),
            smem(lambda i: (t // IDX_BLOCK + i // per,)),
            smem(lambda i: (jnp.minimum(i + 1, nsteps - 1) // per,)),
            smem(lambda i: (t // IDX_BLOCK + jnp.minimum(i + 1, nsteps - 1) // per,)),
            pl.BlockSpec((tm, D_MODEL), row),
            pl.BlockSpec((tm, LANES), row),
            pl.BlockSpec((1, D_MODEL), lambda i: (0, 0)),
            pl.BlockSpec(memory_space=pl.ANY),
        ],
        out_specs=pl.BlockSpec((tm, D_MODEL), row),
        scratch_shapes=[pltpu.VMEM((2, 2 * tm * CHUNKS, LANES), F32),
                        pltpu.SemaphoreType.DMA((2,))],
        compiler_params=pltpu.CompilerParams(
            dimension_semantics=("arbitrary",), vmem_limit_bytes=VMEM_LIMIT),
        name="combine",
    )(dest, dest, dest, dest, x2d, wt, fg, ybuf)


def _rope_tables():
    pos = jnp.arange(SEQ, dtype=F32)
    inv = ROPE_THETA ** (-jnp.arange(0, HEAD_DIM, 2, dtype=F32) / HEAD_DIM)
    ang1 = pos[:, None] * inv[None, :]
    rows = SEQ // GRID_W
    r = jnp.broadcast_to(jnp.arange(rows, dtype=F32)[:, None], (rows, GRID_W)).reshape(-1)
    c = jnp.broadcast_to(jnp.arange(GRID_W, dtype=F32)[None, :], (rows, GRID_W)).reshape(-1)
    axis_dim = HEAD_DIM // 2
    inv2 = ROPE_THETA ** (-jnp.arange(0, axis_dim, 2, dtype=F32) / axis_dim)
    ang2 = jnp.concatenate([r[:, None] * inv2, c[:, None] * inv2], axis=-1)

    def tables(ang):
        cs, sn = jnp.cos(ang), jnp.sin(ang)
        reps = LANES // HEAD_DIM
        return (jnp.tile(jnp.concatenate([cs, cs], axis=1), (1, reps)),
                jnp.tile(jnp.concatenate([-sn, sn], axis=1), (1, reps)))

    return tables(ang1) + tables(ang2)


def kernel(x, mem, mem_norm_g, w_mem_kv, norm1_g, w_in, lam_q1, lam_k1, lam_q2, lam_k2,
           subln_g, q_norm_g, k_norm_g, w_up_a, w_up_b, w_up_c, w_out, norm2_g,
           w_router_group, b_router_group, w_router_expert, b_router_expert,
           w_exp_gate, w_exp_up, w_exp_down, final_norm_g):
    batch, seq, d = x.shape
    depth = w_in.shape[0]
    assert (seq, d, mem.shape[1]) == (SEQ, D_MODEL, MEM_LEN)
    t = batch * seq
    ca, sa, cb, sb = _rope_tables()
    gidx = jnp.arange(512) // HEAD_DIM
    bd = (gidx[:, None] == gidx[None, :]).astype(BF16)
    ti = jnp.arange(TM_POST)
    tri = (ti[:, None] < ti[None, :]).astype(BF16)
    ei = jnp.arange(N_EXPERTS)
    ltri = (ei[None, :] <= ei[:, None]).astype(BF16)
    buf_rows = t * TOP_K + N_EXPERTS * BM
    assert buf_rows // BM <= LANES

    mkv = _memkv(mem.reshape(batch * MEM_LEN, d), mem_norm_g.reshape(1, d), w_mem_kv.astype(BF16))
    x2d = x.reshape(t, d)
    ybuf = None
    for l in range(depth):
        aq, ak, av, bq, bk, bv, cq, gate = _in_proj(
            x2d, norm1_g[l].reshape(1, d), w_in[l].astype(BF16), ca, sa, cb, sb,
            jnp.tile(q_norm_g[l], 8).reshape(1, 512), jnp.tile(k_norm_g[l], 2).reshape(1, LANES),
            bd)

        lam_init = 0.8 - 0.6 * math.exp(-0.3 * l)
        lamp = jnp.stack([lam_q1[l], lam_k1[l], lam_q2[l], lam_k2[l]]).astype(F32)
        oa = _attention(aq, ak, av, batch, diff=True, lamp=lamp,
                        gs=subln_g[l].reshape(LANES, 1), lam_init=lam_init)
        if ybuf is None:
            ob, ybuf = _attention(bq, bk, bv, batch, diff=False, zero_rows=buf_rows * CHUNKS)
        else:
            ob = _attention(bq, bk, bv, batch, diff=False)

        wr = jnp.zeros((ROUTE_ROWS, d), F32)
        wr = wr.at[0:N_GROUPS].set(w_router_group[l].T).at[SUBLANES:].set(w_router_expert[l].T)
        wr_hi = wr.astype(BF16)
        wr_lo = (wr - wr_hi.astype(F32)).astype(BF16)
        br = jnp.zeros((ROUTE_ROWS, 1), F32)
        br = br.at[0:N_GROUPS, 0].set(b_router_group[l]).at[SUBLANES:, 0].set(b_router_expert[l])
        x2d, h2, route, wt, cnt = _post(
            x2d, oa, ob, cq, gate, mkv, w_up_a[l].astype(BF16), w_up_b[l].astype(BF16),
            w_up_c[l].astype(BF16), w_out[l].astype(BF16), norm2_g[l].reshape(1, d),
            jnp.concatenate([wr_hi, wr_lo], axis=0), br, tri)

        slots, plan = _plan(cnt, route, ltri, BM)
        slots = slots[:TOP_K].reshape(-1)
        nblk = buf_rows // BM
        xbuf = _dispatch(slots, h2, ybuf)
        ybuf = _moe(plan[0, :nblk], plan[1, :nblk], plan[2, :1], xbuf,
                    w_exp_gate, w_exp_up, w_exp_down, l, BM)
        x2d = _combine(slots, x2d, wt, final_norm_g.reshape(1, d), ybuf,
                       final=(l == depth - 1))
    return x2d.reshape(batch, seq, d)
```

```python
import functools
import math

import jax
import jax.numpy as jnp
from jax import lax
from jax.experimental import pallas as pl
from jax.experimental.pallas import tpu as pltpu

F32 = jnp.float32
BF16 = jnp.bfloat16

D_MODEL = 1024
SEQ = 2048
MEM_LEN = 256
HEAD_DIM = 64
MX_HEAD_DIM = 128
BRANCH_W = 512
GRID_W = 64
ROPE_THETA = 10000.0
NORM_EPS = 1e-6
N_GROUPS = 4
EXPERTS_PER_GROUP = 8
N_EXPERTS = N_GROUPS * EXPERTS_PER_GROUP
TOP_K = 2
D_EXPERT = 512

LANES = 128
SUBLANES = 8
CHUNKS = D_MODEL // LANES

TM_IN = 512
TQ = 2048
TK = 256
ONES_ROWS = 16
LOG2E = math.log2(math.e)
TM_POST = 512
BM = 512
TM_COMB = 256
TD = 2048
ISSUE_UNROLL = 16
IDX_BLOCK = 1024
VMEM_LIMIT = 56 * 1024 * 1024

C_AQ, C_AK, C_AV, C_BQ, C_BK, C_BV, C_CQ, C_G, C_END = (
    0, 512, 1024, 1536, 2048, 2176, 2304, 2816, 5888)
ROUTE_ROWS = 40


def _rms(xf, g):
    ms = jnp.mean(xf * xf, axis=-1, keepdims=True)
    return xf * lax.rsqrt(ms + NORM_EPS) * g


def _dot(a, b):
    return jnp.dot(a, b, preferred_element_type=F32)


def _dot_nt(a, b):
    return lax.dot_general(a, b, (((1,), (1,)), ((), ())), preferred_element_type=F32)


def _memkv_kernel(m_ref, g_ref, w_ref, o_ref):
    h = _rms(m_ref[...], g_ref[...]).astype(BF16)
    o_ref[...] = _dot(h, w_ref[...]).astype(BF16)


def _memkv(mem2d, g, w):
    n = mem2d.shape[0]
    tm = 512
    return pl.pallas_call(
        _memkv_kernel,
        out_shape=jax.ShapeDtypeStruct((n, w.shape[1]), BF16),
        grid=(n // tm,),
        in_specs=[pl.BlockSpec((tm, D_MODEL), lambda i: (i, 0)),
                  pl.BlockSpec((1, D_MODEL), lambda i: (0, 0)),
                  pl.BlockSpec(w.shape, lambda i: (0, 0))],
        out_specs=pl.BlockSpec((tm, w.shape[1]), lambda i: (i, 0)),
        compiler_params=pltpu.CompilerParams(
            dimension_semantics=("arbitrary",), vmem_limit_bytes=VMEM_LIMIT),
        name="memkv",
    )(mem2d, g, w)


def _in_kernel(x_ref, g1_ref, w_ref, ca_ref, sa_ref, cb_ref, sb_ref, qg_ref, kg_ref, bd_ref,
               aq_ref, ak_ref, av_ref, bq_ref, bk_ref, bv_ref, cq_ref, gate_ref):
    tm = x_ref.shape[0]
    h = _rms(x_ref[...], g1_ref[...]).astype(BF16)
    lane = lax.broadcasted_iota(jnp.int32, (tm, LANES), 1)
    first_half = (lane & (HEAD_DIM // 2)) == 0

    def seg(lo, hi):
        return _dot(h, w_ref[:, lo:hi])

    def rope(p, c, s):
        sw = jnp.where(first_half, pltpu.roll(p, LANES - HEAD_DIM // 2, 1),
                       pltpu.roll(p, HEAD_DIM // 2, 1))
        return p * c + sw * s

    def group_norm(p, gain):
        n = p.shape[1]
        ss = _dot((p * p).astype(BF16), bd_ref[:n, :n])
        return p * lax.rsqrt(ss * (1.0 / HEAD_DIM) + NORM_EPS) * gain

    def rope_store(p, c_ref, s_ref, o_ref, scale):
        c = c_ref[...]
        s = s_ref[...]
        for j in range(p.shape[1] // LANES):
            sl = slice(j * LANES, (j + 1) * LANES)
            o_ref[:, sl] = (rope(p[:, sl], c, s) * scale).astype(BF16)

    q_scale = HEAD_DIM ** -0.5 * LOG2E
    rope_store(seg(C_AQ, C_AK), ca_ref, sa_ref, aq_ref, q_scale)
    rope_store(seg(C_AK, C_AV), ca_ref, sa_ref, ak_ref, 1.0)
    av_ref[...] = seg(C_AV, C_BQ).T.astype(BF16)
    rope_store(group_norm(seg(C_BQ, C_BK), qg_ref[...]), cb_ref, sb_ref, bq_ref, q_scale)

    lo_half = lane < HEAD_DIM

    def both_halves(p):
        sw = pltpu.roll(p, HEAD_DIM, 1)
        return jnp.where(lo_half, p, sw), jnp.where(lo_half, sw, p)

    kb = rope(group_norm(seg(C_BK, C_BV), kg_ref[...]), cb_ref[...], sb_ref[...])
    for j, kj in enumerate(both_halves(kb)):
        bk_ref[:, j * LANES:(j + 1) * LANES] = kj.astype(BF16)
    for j, vj in enumerate(both_halves(seg(C_BV, C_CQ))):
        bv_ref[j * LANES:(j + 1) * LANES, :] = vj.T.astype(BF16)
    cq_ref[...] = (seg(C_CQ, C_G) * (MX_HEAD_DIM ** -0.5)).astype(BF16)
    for j in range((C_END - C_G) // 512):
        lo = C_G + j * 512
        z = seg(lo, lo + 512)
        gate_ref[:, j * 512:(j + 1) * 512] = (1.0 / (1.0 + jnp.exp(-z))).astype(BF16)


def _in_proj(x2d, g1, w, ca, sa, cb, sb, qg, kg, bd):
    t = x2d.shape[0]
    tm = TM_IN
    nrb = SEQ // tm
    row = lambda i: (i, 0)
    const = lambda i: (0, 0)
    tab = lambda i: (i % nrb, 0)
    outs = ((512, False), (512, False), (512, True), (512, False), (256, False), (256, True),
            (512, False), (3072, False))
    col = lambda i: (0, i)
    return pl.pallas_call(
        _in_kernel,
        out_shape=[jax.ShapeDtypeStruct((n, t) if tr else (t, n), BF16) for n, tr in outs],
        grid=(t // tm,),
        in_specs=[pl.BlockSpec((tm, D_MODEL), row),
                  pl.BlockSpec((1, D_MODEL), const),
                  pl.BlockSpec(w.shape, const),
                  pl.BlockSpec((tm, LANES), tab), pl.BlockSpec((tm, LANES), tab),
                  pl.BlockSpec((tm, LANES), tab), pl.BlockSpec((tm, LANES), tab),
                  pl.BlockSpec((1, 512), const), pl.BlockSpec((1, LANES), const),
                  pl.BlockSpec((512, 512), const)],
        out_specs=[pl.BlockSpec((n, tm), col) if tr else pl.BlockSpec((tm, n), row)
                   for n, tr in outs],
        compiler_params=pltpu.CompilerParams(
            dimension_semantics=("arbitrary",), vmem_limit_bytes=VMEM_LIMIT),
        name="in_proj",
    )(x2d, g1, w, ca, sa, cb, sb, qg, kg, bd)


def _attn_kernel(*refs, diff, post_scale, lam_init, zero_fill=False):
    if diff:
        lamp_ref, gs_ref, q_ref, k_ref, vt_ref, o_ref = refs
    elif zero_fill:
        q_ref, k_ref, vt_ref, o_ref, z_ref = refs
        z_ref[...] = jnp.zeros_like(z_ref)
    else:
        q_ref, k_ref, vt_ref, o_ref = refs
    tq = q_ref.shape[0]
    q = q_ref[...]
    lane = lax.broadcasted_iota(jnp.int32, (tq, LANES), 1)
    lo = lane < HEAD_DIM
    zero = jnp.zeros_like(q)
    qs = jnp.concatenate([jnp.where(lo, q, zero), jnp.where(lo, zero, q)], axis=0)
    cols = 2 * tq
    ones = jnp.ones((ONES_ROWS, TK), BF16)
    m = jnp.full((1, cols), -jnp.inf, F32)
    acc = jnp.zeros((LANES + ONES_ROWS, cols), F32)
    nchunks = SEQ // TK

    def scores(j):
        return _dot_nt(k_ref[j * TK:(j + 1) * TK, :], qs)

    st_next = scores(0)
    for j in range(nchunks):
        st = st_next
        if j + 1 < nchunks:
            st_next = scores(j + 1)
        vtj = jnp.concatenate([vt_ref[:, j * TK:(j + 1) * TK], ones], axis=0)
        m_new = jnp.maximum(m, jnp.max(st, axis=0, keepdims=True))
        alpha = jnp.exp2(m - m_new)
        e = jnp.exp2(st - m_new).astype(BF16)
        acc = alpha * acc + _dot(vtj, e)
        m = m_new
    o = acc[:LANES] / acc[LANES:LANES + 1]
    if diff:
        lp = lamp_ref[...]
        lam = (jnp.exp(jnp.sum(lp[0:1] * lp[1:2], axis=-1, keepdims=True))
               - jnp.exp(jnp.sum(lp[2:3] * lp[3:4], axis=-1, keepdims=True)) + lam_init)
        d = o[:, :tq] - lam * o[:, tq:]
        ms = jnp.mean(d * d, axis=0, keepdims=True)
        out_t = d * lax.rsqrt(ms + NORM_EPS) * gs_ref[...] * post_scale
    else:
        row = lax.broadcasted_iota(jnp.int32, (LANES, tq), 0)
        out_t = jnp.where(row < HEAD_DIM, o[:, :tq], o[:, tq:])
    o_ref[...] = out_t.T.astype(BF16)


def _attention(q, k, vt, batch, *, diff, lamp=None, gs=None, lam_init=0.0, zero_rows=0):
    t = q.shape[0]
    nq = SEQ // TQ
    nblk = q.shape[1] // LANES
    kv_per = nblk // (k.shape[1] // LANES)
    qmap = lambda b, h, i: (b * nq + i, h)
    out_shape = jax.ShapeDtypeStruct((t, q.shape[1]), BF16)
    out_specs = pl.BlockSpec((TQ, LANES), qmap)
    if zero_rows:
        zblk = zero_rows // (batch * nblk * nq)
        out_shape = [out_shape, jax.ShapeDtypeStruct((zero_rows, LANES), F32)]
        out_specs = [out_specs,
                     pl.BlockSpec((zblk, LANES), lambda b, h, i: ((b * nblk + h) * nq + i, 0))]
    in_specs = [pl.BlockSpec((TQ, LANES), qmap),
                pl.BlockSpec((SEQ, LANES), lambda b, h, i: (b, h // kv_per)),
                pl.BlockSpec((LANES, SEQ), lambda b, h, i: (h // kv_per, b))]
    args = [q, k, vt]
    if diff:
        const = lambda b, h, i: (0, 0)
        in_specs = [pl.BlockSpec((4, HEAD_DIM), const), pl.BlockSpec((LANES, 1), const)] + in_specs
        args = [lamp, gs] + args
    return pl.pallas_call(
        functools.partial(_attn_kernel, diff=diff, post_scale=1.0 - lam_init, lam_init=lam_init,
                          zero_fill=bool(zero_rows)),
        out_shape=out_shape,
        grid=(batch, nblk, nq),
        in_specs=in_specs,
        out_specs=out_specs,
        compiler_params=pltpu.CompilerParams(
            dimension_semantics=("arbitrary", "arbitrary", "arbitrary"),
            vmem_limit_bytes=VMEM_LIMIT),
        name="attn_diff" if diff else "attn_gqa",
    )(*args)


def _post_kernel(x_ref, oa_ref, ob_ref, cq_ref, gate_ref, mkv_ref, wa_ref, wb_ref, wc_ref,
                 wo_ref, g2_ref, wr_ref, br_ref, tri_ref,
                 xo_ref, h2_ref, route_ref, wt_ref, cnt_ref, carry_ref):
    tm = x_ref.shape[0]
    i = pl.program_id(0)

    @pl.when(i == 0)
    def _():
        carry_ref[...] = jnp.zeros_like(carry_ref)

    heads = []
    for hd in range(BRANCH_W // MX_HEAD_DIM):
        sl = slice(hd * MX_HEAD_DIM, (hd + 1) * MX_HEAD_DIM)
        sv = slice(BRANCH_W + hd * MX_HEAD_DIM, BRANCH_W + (hd + 1) * MX_HEAD_DIM)
        s = _dot_nt(cq_ref[:, sl], mkv_ref[:, sl])
        e = jnp.exp(s - jnp.max(s, axis=-1, keepdims=True))
        den = jnp.sum(e, axis=-1, keepdims=True)
        heads.append((_dot(e.astype(BF16), mkv_ref[:, sv]) / den).astype(BF16))
    oc = jnp.concatenate(heads, axis=1)

    ya = _dot(oa_ref[...], wa_ref[...])
    yb = _dot(ob_ref[...], wb_ref[...])
    yc = _dot(oc, wc_ref[...])
    merged = (gate_ref[:, 0:D_MODEL].astype(F32) * ya
              + gate_ref[:, D_MODEL:2 * D_MODEL].astype(F32) * yb
              + gate_ref[:, 2 * D_MODEL:3 * D_MODEL].astype(F32) * yc)
    xn = x_ref[...] + _dot(merged.astype(BF16), wo_ref[...])
    xo_ref[...] = xn
    h2 = _rms(xn, g2_ref[...])
    for c in range(CHUNKS):
        h2_ref[pl.ds(c, tm, stride=CHUNKS), :] = h2[:, c * LANES:(c + 1) * LANES]

    h_hi = h2.astype(BF16)
    h_lo = (h2 - h_hi.astype(F32)).astype(BF16)
    l2 = _dot_nt(wr_ref[...], h_hi)
    logits = (l2[:ROUTE_ROWS] + l2[ROUTE_ROWS:] + _dot_nt(wr_ref[:ROUTE_ROWS, :], h_lo)
              + br_ref[...])

    neg = -jnp.inf
    r8 = lax.broadcasted_iota(jnp.int32, (SUBLANES, tm), 0)
    r32 = lax.broadcasted_iota(jnp.int32, (N_EXPERTS, tm), 0)
    gl = jnp.where(r8 < N_GROUPS, logits[0:SUBLANES], neg)
    gmax = jnp.max(gl, axis=0, keepdims=True)
    gidx = jnp.min(jnp.where(gl == gmax, r8, SUBLANES), axis=0, keepdims=True)
    gp = 1.0 / jnp.sum(jnp.exp(gl - gmax), axis=0, keepdims=True)
    el = jnp.where((r32 // EXPERTS_PER_GROUP) == gidx, logits[SUBLANES:ROUTE_ROWS], neg)
    m1 = jnp.max(el, axis=0, keepdims=True)
    i1 = jnp.min(jnp.where(el == m1, r32, N_EXPERTS), axis=0, keepdims=True)
    el2 = jnp.where(r32 == i1, neg, el)
    m2 = jnp.max(el2, axis=0, keepdims=True)
    i2 = jnp.min(jnp.where(el2 == m2, r32, N_EXPERTS), axis=0, keepdims=True)
    d = jnp.exp(m2 - m1)
    w1 = gp / (1.0 + d)
    w2 = gp * d / (1.0 + d)

    hit1 = r32 == i1
    hit2 = r32 == i2
    oh = jnp.where(hit1 | hit2, 1.0, 0.0)
    before = _dot(oh.astype(BF16), tri_ref[...]) + carry_ref[...]
    rank1 = jnp.sum(jnp.where(hit1, before, 0.0), axis=0, keepdims=True)
    rank2 = jnp.sum(jnp.where(hit2, before, 0.0), axis=0, keepdims=True)
    carry_ref[...] = carry_ref[...] + jnp.sum(oh, axis=1, keepdims=True)
    zrow = jnp.zeros_like(w1)
    route = jnp.concatenate(
        [i1.astype(F32), i2.astype(F32), rank1, rank2, w1, w2, zrow, zrow], axis=0)
    route_ref[...] = route
    wt_ref[...] = jnp.concatenate(
        [route, jnp.zeros((LANES - SUBLANES, tm), F32)], axis=0).T
    cnt_ref[...] = jnp.broadcast_to(carry_ref[...], cnt_ref.shape)


def _post(x2d, oa, ob, cq, gate, mkv, wa, wb, wc, wo, g2, wr, br, tri):
    t = x2d.shape[0]
    tm = TM_POST
    nrb = SEQ // tm
    row = lambda i: (i, 0)
    const = lambda i: (0, 0)
    return pl.pallas_call(
        _post_kernel,
        out_shape=[jax.ShapeDtypeStruct((t, D_MODEL), F32),
                   jax.ShapeDtypeStruct((t * CHUNKS, LANES), F32),
                   jax.ShapeDtypeStruct((SUBLANES, t), F32),
                   jax.ShapeDtypeStruct((t, LANES), F32),
                   jax.ShapeDtypeStruct((N_EXPERTS, LANES), F32)],
        grid=(t // tm,),
        in_specs=[pl.BlockSpec((tm, D_MODEL), row),
                  pl.BlockSpec((tm, BRANCH_W), row),
                  pl.BlockSpec((tm, BRANCH_W), row),
                  pl.BlockSpec((tm, BRANCH_W), row),
                  pl.BlockSpec((tm, 3 * D_MODEL), row),
                  pl.BlockSpec((MEM_LEN, 2 * BRANCH_W), lambda i: (i // nrb, 0)),
                  pl.BlockSpec(wa.shape, const), pl.BlockSpec(wb.shape, const),
                  pl.BlockSpec(wc.shape, const), pl.BlockSpec(wo.shape, const),
                  pl.BlockSpec((1, D_MODEL), const),
                  pl.BlockSpec(wr.shape, const), pl.BlockSpec(br.shape, const),
                  pl.BlockSpec(tri.shape, const)],
        out_specs=[pl.BlockSpec((tm, D_MODEL), row),
                   pl.BlockSpec((tm * CHUNKS, LANES), row),
                   pl.BlockSpec((SUBLANES, tm), lambda i: (0, i)),
                   pl.BlockSpec((tm, LANES), row),
                   pl.BlockSpec((N_EXPERTS, LANES), const)],
        scratch_shapes=[pltpu.VMEM((N_EXPERTS, 1), F32)],
        compiler_params=pltpu.CompilerParams(
            dimension_semantics=("arbitrary",), vmem_limit_bytes=VMEM_LIMIT),
        name="post",
    )(x2d, oa, ob, cq, gate, mkv, wa, wb, wc, wo, g2, wr, br, tri)


def _plan_kernel(cnt_ref, route_ref, ltri_ref, slots_ref, plan_ref, *, bm):
    t = route_ref.shape[1]
    counts = cnt_ref[...]
    nblk_e = jnp.floor((counts + (bm - 1)) * (1.0 / bm))
    cum = _dot(ltri_ref[...], nblk_e.astype(BF16))
    first = cum - nblk_e
    first_row = first[:, 0:1] * bm
    e_tok = lax.broadcasted_iota(jnp.int32, (N_EXPERTS, t), 0)
    rows = []
    for k in range(TOP_K):
        e_k = route_ref[k:k + 1, :].astype(jnp.int32)
        base = jnp.sum(jnp.where(e_tok == e_k, first_row, 0.0), axis=0, keepdims=True)
        rows.append((base + route_ref[TOP_K + k:TOP_K + k + 1, :]).astype(jnp.int32))
    slots_ref[...] = jnp.concatenate(
        rows + [jnp.zeros((SUBLANES - TOP_K, t), jnp.int32)], axis=0)

    nb_used = cum[N_EXPERTS - 1:N_EXPERTS, :]
    b = lax.broadcasted_iota(jnp.int32, (1, LANES), 1).astype(F32)
    bidx = jnp.minimum(b, nb_used - 1.0)
    bexp = jnp.sum(jnp.where(cum <= bidx, 1.0, 0.0), axis=0, keepdims=True)
    sel = lax.broadcasted_iota(jnp.int32, (N_EXPERTS, LANES), 0).astype(F32) == bexp
    cnt_at = jnp.sum(jnp.where(sel, counts, 0.0), axis=0, keepdims=True)
    first_at = jnp.sum(jnp.where(sel, first, 0.0), axis=0, keepdims=True)
    bval = jnp.clip(cnt_at - (bidx - first_at) * bm, 0.0, float(bm))
    plan_ref[...] = jnp.concatenate(
        [bexp, bval, nb_used, jnp.zeros((SUBLANES - 3, LANES), F32)], axis=0).astype(jnp.int32)


def _plan(cnt, route, ltri, bm):
    t = route.shape[1]
    full = lambda shape: pl.BlockSpec(shape, lambda i: (0, 0))
    return pl.pallas_call(
        functools.partial(_plan_kernel, bm=bm),
        out_shape=[jax.ShapeDtypeStruct((SUBLANES, t), jnp.int32),
                   jax.ShapeDtypeStruct((SUBLANES, LANES), jnp.int32)],
        grid=(1,),
        in_specs=[full(cnt.shape), full(route.shape), full(ltri.shape)],
        out_specs=[full((SUBLANES, t)), full((SUBLANES, LANES))],
        compiler_params=pltpu.CompilerParams(
            dimension_semantics=("arbitrary",), vmem_limit_bytes=VMEM_LIMIT),
        name="plan",
    )(cnt, route, ltri)


def _start_row_gather(idx_refs, off, n, src_ref, dst_ref, slot, sem):
    def body(i, carry):
        for u in range(ISSUE_UNROLL // TOP_K):
            r = i * (ISSUE_UNROLL // TOP_K) + u
            for k in range(TOP_K):
                row = idx_refs[k][off + r]
                pltpu.make_async_copy(
                    src_ref.at[pl.ds(pl.multiple_of(row * CHUNKS, CHUNKS), CHUNKS)],
                    dst_ref.at[slot, pl.ds(pl.multiple_of((k * n + r) * CHUNKS, CHUNKS), CHUNKS)],
                    sem.at[slot]).start(priority=k)
        return carry
    lax.fori_loop(0, n // (ISSUE_UNROLL // TOP_K), body, 0)


def _wait_row_gather(n, src_ref, dst_ref, slot, sem):
    pltpu.make_async_copy(src_ref.at[pl.ds(0, n * CHUNKS)], dst_ref.at[slot], sem.at[slot]).wait()


def _rows_from_tiles(buf_ref, slot, first, n):
    return jnp.concatenate(
        [buf_ref[slot, pl.ds(first * CHUNKS + c, n, stride=CHUNKS), :] for c in range(CHUNKS)],
        axis=1)


def _dispatch_kernel(idx0_ref, idx1_ref, h2_ref, xin_ref, xbuf_ref, sem):
    del xin_ref
    td = idx0_ref.shape[0]
    idx_refs = (idx0_ref, idx1_ref)

    def body(i, carry):
        for u in range(ISSUE_UNROLL // TOP_K):
            tok = i * (ISSUE_UNROLL // TOP_K) + u
            for k in range(TOP_K):
                slot_row = idx_refs[k][tok]
                pltpu.make_async_copy(
                    h2_ref.at[pl.ds(pl.multiple_of(tok * CHUNKS, CHUNKS), CHUNKS)],
                    xbuf_ref.at[pl.ds(pl.multiple_of(slot_row * CHUNKS, CHUNKS), CHUNKS)],
                    sem.at[0]).start(priority=k)
        return carry
    lax.fori_loop(0, td // (ISSUE_UNROLL // TOP_K), body, 0)
    for _ in range(TOP_K):
        pltpu.make_async_copy(h2_ref, xbuf_ref.at[pl.ds(0, h2_ref.shape[0])], sem.at[0]).wait()


def _dispatch(slots, h2, xbuf_init):
    td = TD
    nsteps = slots.shape[0] // (TOP_K * td)
    return pl.pallas_call(
        _dispatch_kernel,
        out_shape=jax.ShapeDtypeStruct(xbuf_init.shape, F32),
        grid=(nsteps,),
        in_specs=[pl.BlockSpec((td,), lambda i: (i,), memory_space=pltpu.SMEM),
                  pl.BlockSpec((td,), lambda i: (nsteps + i,), memory_space=pltpu.SMEM),
                  pl.BlockSpec((td * CHUNKS, LANES), lambda i: (i, 0)),
                  pl.BlockSpec(memory_space=pl.ANY)],
        out_specs=pl.BlockSpec(memory_space=pl.ANY),
        input_output_aliases={3: 0},
        scratch_shapes=[pltpu.SemaphoreType.DMA((1,))],
        compiler_params=pltpu.CompilerParams(
            dimension_semantics=("arbitrary",), vmem_limit_bytes=VMEM_LIMIT),
        name="dispatch",
    )(slots, slots, h2, xbuf_init)


def _moe_kernel(bexp_ref, bval_ref, nb_ref, x_ref, wg_ref, wu_ref, wd_ref, y_ref,
                wgb, wub, wdb):
    b = pl.program_id(0)

    @pl.when((b == 0) | (bexp_ref[b] != bexp_ref[jnp.maximum(b - 1, 0)]))
    def _():
        wgb[...] = wg_ref[0, 0].astype(BF16)
        wub[...] = wu_ref[0, 0].astype(BF16)
        wdb[...] = wd_ref[0, 0].astype(BF16)

    @pl.when(b >= nb_ref[0])
    def _():
        y_ref[...] = jnp.zeros_like(y_ref)

    @pl.when(b < nb_ref[0])
    def _():
        bm = x_ref.shape[0] // CHUNKS
        xf = jnp.concatenate(
            [x_ref[pl.ds(c, bm, stride=CHUNKS), :] for c in range(CHUNKS)], axis=1)
        rowid = lax.broadcasted_iota(jnp.int32, (bm, 1), 0)
        xb = jnp.where(rowid < bval_ref[b], xf, 0.0).astype(BF16)
        hg = _dot(xb, wgb[...])
        hu = _dot(xb, wub[...])
        hid = (hg / (1.0 + jnp.exp(-hg)) * hu).astype(BF16)
        y = _dot(hid, wdb[...])
        for c in range(CHUNKS):
            y_ref[pl.ds(c, bm, stride=CHUNKS), :] = y[:, c * LANES:(c + 1) * LANES]


def _moe(bexp, bval, nb_used, xbuf, wg, wu, wd, layer, bm):
    nblk = bexp.shape[0]
    wmap = lambda b, bexp, bval, nb: (layer, bexp[b], 0, 0)
    xmap = lambda b, bexp, bval, nb: (jnp.minimum(b, nb[0] - 1), 0)
    return pl.pallas_call(
        _moe_kernel,
        out_shape=jax.ShapeDtypeStruct(xbuf.shape, F32),
        grid_spec=pltpu.PrefetchScalarGridSpec(
            num_scalar_prefetch=3,
            grid=(nblk,),
            in_specs=[
                pl.BlockSpec((bm * CHUNKS, LANES), xmap),
                pl.BlockSpec((1, 1, D_MODEL, D_EXPERT), wmap),
                pl.BlockSpec((1, 1, D_MODEL, D_EXPERT), wmap),
                pl.BlockSpec((1, 1, D_EXPERT, D_MODEL), wmap),
            ],
            out_specs=pl.BlockSpec((bm * CHUNKS, LANES), lambda b, bexp, bval, nb: (b, 0)),
            scratch_shapes=[pltpu.VMEM((D_MODEL, D_EXPERT), BF16),
                            pltpu.VMEM((D_MODEL, D_EXPERT), BF16),
                            pltpu.VMEM((D_EXPERT, D_MODEL), BF16)]),
        compiler_params=pltpu.CompilerParams(
            dimension_semantics=("arbitrary",), vmem_limit_bytes=VMEM_LIMIT),
        name="experts",
    )(bexp, bval, nb_used, xbuf, wg, wu, wd)


def _comb_kernel(cur0_ref, cur1_ref, nxt0_ref, nxt1_ref, x_ref, wt_ref, fg_ref, y_ref, o_ref,
                 ybuf, sem, *, final):
    tm = x_ref.shape[0]
    i = pl.program_id(0)
    n = pl.num_programs(0)
    slot = i % 2
    per = IDX_BLOCK // tm

    @pl.when(i == 0)
    def _():
        _start_row_gather((cur0_ref, cur1_ref), 0, tm, y_ref, ybuf, 0, sem)

    @pl.when(i + 1 < n)
    def _():
        _start_row_gather((nxt0_ref, nxt1_ref), ((i + 1) % per) * tm, tm, y_ref, ybuf,
                          1 - slot, sem)

    def compute(s):
        _wait_row_gather(2 * tm, y_ref, ybuf, s, sem)
        y0 = _rows_from_tiles(ybuf, s, 0, tm)
        y1 = _rows_from_tiles(ybuf, s, tm, tm)
        wt = wt_ref[...]
        xo = x_ref[...] + wt[:, 4:5] * y0 + wt[:, 5:6] * y1
        if final:
            xo = _rms(xo, fg_ref[...])
        o_ref[...] = xo

    for s in range(2):
        @pl.when(slot == s)
        def _(s=s):
            compute(s)


def _combine(dest, x2d, wt, fg, ybuf, *, final):
    t = x2d.shape[0]
    tm = TM_COMB
    nsteps = t // tm
    per = IDX_BLOCK // tm
    row = lambda i: (i, 0)
    smem = lambda index_map: pl.BlockSpec((IDX_BLOCK,), index_map, memory_space=pltpu.SMEM)
    return pl.pallas_call(
        functools.partial(_comb_kernel, final=final),
        out_shape=jax.ShapeDtypeStruct((t, D_MODEL), F32),
        grid=(nsteps,),
        in_specs=[
            smem(lambda i: (i // per,)),
            smem(lambda i: (t // IDX_BLOCK + i // per,)),
            smem(lambda i: (jnp.minimum(i + 1, nsteps - 1) // per,)),
            smem(lambda i: (t // IDX_BLOCK + jnp.minimum(i + 1, nsteps - 1) // per,)),
            pl.BlockSpec((tm, D_MODEL), row),
            pl.BlockSpec((tm, LANES), row),
            pl.BlockSpec((1, D_MODEL), lambda i: (0, 0)),
            pl.BlockSpec(memory_space=pl.ANY),
        ],
        out_specs=pl.BlockSpec((tm, D_MODEL), row),
        scratch_shapes=[pltpu.VMEM((2, 2 * tm * CHUNKS, LANES), F32),
                        pltpu.SemaphoreType.DMA((2,))],
        compiler_params=pltpu.CompilerParams(
            dimension_semantics=("arbitrary",), vmem_limit_bytes=VMEM_LIMIT),
        name="combine",
    )(dest, dest, dest, dest, x2d, wt, fg, ybuf)


def _rope_tables():
    pos = jnp.arange(SEQ, dtype=F32)
    inv = ROPE_THETA ** (-jnp.arange(0, HEAD_DIM, 2, dtype=F32) / HEAD_DIM)
    ang1 = pos[:, None] * inv[None, :]
    rows = SEQ // GRID_W
    r = jnp.broadcast_to(jnp.arange(rows, dtype=F32)[:, None], (rows, GRID_W)).reshape(-1)
    c = jnp.broadcast_to(jnp.arange(GRID_W, dtype=F32)[None, :], (rows, GRID_W)).reshape(-1)
    axis_dim = HEAD_DIM // 2
    inv2 = ROPE_THETA ** (-jnp.arange(0, axis_dim, 2, dtype=F32) / axis_dim)
    ang2 = jnp.concatenate([r[:, None] * inv2, c[:, None] * inv2], axis=-1)

    def tables(ang):
        cs, sn = jnp.cos(ang), jnp.sin(ang)
        reps = LANES // HEAD_DIM
        return (jnp.tile(jnp.concatenate([cs, cs], axis=1), (1, reps)),
                jnp.tile(jnp.concatenate([-sn, sn], axis=1), (1, reps)))

    return tables(ang1) + tables(ang2)


def kernel(x, mem, mem_norm_g, w_mem_kv, norm1_g, w_in, lam_q1, lam_k1, lam_q2, lam_k2,
           subln_g, q_norm_g, k_norm_g, w_up_a, w_up_b, w_up_c, w_out, norm2_g,
           w_router_group, b_router_group, w_router_expert, b_router_expert,
           w_exp_gate, w_exp_up, w_exp_down, final_norm_g):
    batch, seq, d = x.shape
    depth = w_in.shape[0]
    assert (seq, d, mem.shape[1]) == (SEQ, D_MODEL, MEM_LEN)
    t = batch * seq
    ca, sa, cb, sb = _rope_tables()
    gidx = jnp.arange(512) // HEAD_DIM
    bd = (gidx[:, None] == gidx[None, :]).astype(BF16)
    ti = jnp.arange(TM_POST)
    tri = (ti[:, None] < ti[None, :]).astype(BF16)
    ei = jnp.arange(N_EXPERTS)
    ltri = (ei[None, :] <= ei[:, None]).astype(BF16)
    buf_rows = t * TOP_K + N_EXPERTS * BM
    assert buf_rows // BM <= LANES

    mkv = _memkv(mem.reshape(batch * MEM_LEN, d), mem_norm_g.reshape(1, d), w_mem_kv.astype(BF16))
    x2d = x.reshape(t, d)
    ybuf = None
    for l in range(depth):
        aq, ak, av, bq, bk, bv, cq, gate = _in_proj(
            x2d, norm1_g[l].reshape(1, d), w_in[l].astype(BF16), ca, sa, cb, sb,
            jnp.tile(q_norm_g[l], 8).reshape(1, 512), jnp.tile(k_norm_g[l], 2).reshape(1, LANES),
            bd)

        lam_init = 0.8 - 0.6 * math.exp(-0.3 * l)
        lamp = jnp.stack([lam_q1[l], lam_k1[l], lam_q2[l], lam_k2[l]]).astype(F32)
        oa = _attention(aq, ak, av, batch, diff=True, lamp=lamp,
                        gs=subln_g[l].reshape(LANES, 1), lam_init=lam_init)
        if ybuf is None:
            ob, ybuf = _attention(bq, bk, bv, batch, diff=False, zero_rows=buf_rows * CHUNKS)
        else:
            ob = _attention(bq, bk, bv, batch, diff=False)

        wr = jnp.zeros((ROUTE_ROWS, d), F32)
        wr = wr.at[0:N_GROUPS].set(w_router_group[l].T).at[SUBLANES:].set(w_router_expert[l].T)
        wr_hi = wr.astype(BF16)
        wr_lo = (wr - wr_hi.astype(F32)).astype(BF16)
        br = jnp.zeros((ROUTE_ROWS, 1), F32)
        br = br.at[0:N_GROUPS, 0].set(b_router_group[l]).at[SUBLANES:, 0].set(b_router_expert[l])
        x2d, h2, route, wt, cnt = _post(
            x2d, oa, ob, cq, gate, mkv, w_up_a[l].astype(BF16), w_up_b[l].astype(BF16),
            w_up_c[l].astype(BF16), w_out[l].astype(BF16), norm2_g[l].reshape(1, d),
            jnp.concatenate([wr_hi, wr_lo], axis=0), br, tri)

        slots, plan = _plan(cnt, route, ltri, BM)
        slots = slots[:TOP_K].reshape(-1)
        nblk = buf_rows // BM
        xbuf = _dispatch(slots, h2, ybuf)
        ybuf = _moe(plan[0, :nblk], plan[1, :nblk], plan[2, :1], xbuf,
                    w_exp_gate, w_exp_up, w_exp_down, l, BM)
        x2d = _combine(slots, x2d, wt, final_norm_g.reshape(1, d), ybuf,
                       final=(l == depth - 1))
    return x2d.reshape(batch, seq, d)
```
